```python
import math
import jax
import jax.numpy as jnp
from jax import lax
import numpy as np

D_MODEL = 1024
BATCH = 16
SEQ = 2048
DEPTH = 2
DEC_BATCH = 32
DEC_SEQ = 1
PAST_LEN = 16384
PAGE_SIZE = 128

N_HEADS = 8
HEAD_DIM = D_MODEL // N_HEADS
ATT_DIM = N_HEADS * HEAD_DIM
ROPE_DIM = HEAD_DIM // 4
ROPE_THETA = 500000.0
MOBA_BLOCK = 256
MOBA_TOPK = 3
MOBA_QCHUNK = 16
FOX_QBLOCK = 128
FORGET_BIAS = 4.0
D_FF = 2816
N_EXPERTS = 8
MOE_TOPK = 2
D_FF_EXPERT = 3584
MOE_ROW_BLOCK = 256
RMS_EPS = 1e-6
N_MOBA_LAYERS = (DEPTH + 1) // 2
N_FOX_LAYERS = DEPTH // 2

kernel_name = 'hybrid_moba_fox_decoder_step'


def rms_norm(x, g):
    xf = x.astype(jnp.float32)
    y = xf * lax.rsqrt(jnp.mean(xf * xf, axis=-1, keepdims=True) + RMS_EPS)
    return (y * g.astype(jnp.float32)).astype(x.dtype)


def rope_partial(x, pos):
    half = ROPE_DIM // 2
    inv = jnp.power(jnp.float32(ROPE_THETA), -jnp.arange(half, dtype=jnp.float32) / half)
    ang = pos.astype(jnp.float32)[:, None] * inv[None, :]
    cos = jnp.cos(ang)[None, :, None, :]
    sin = jnp.sin(ang)[None, :, None, :]
    xf = x.astype(jnp.float32)
    x1 = xf[..., :half]
    x2 = xf[..., half:ROPE_DIM]
    rot = jnp.concatenate([x1 * cos - x2 * sin, x2 * cos + x1 * sin], axis=-1)
    return jnp.concatenate([rot.astype(x.dtype), x[..., ROPE_DIM:]], axis=-1)


def gather_pages(pool, layer, page_table):
    rows = pool[layer, page_table]
    b, n_pages = page_table.shape
    return rows.reshape((b, n_pages * pool.shape[2]) + pool.shape[3:])


def moba_attend(q, kb, vb, pos0):
    B, T, H, hd = q.shape
    nb = kb.shape[1]
    k_sel = min(MOBA_TOPK, nb)
    qc = math.gcd(T, MOBA_QCHUNK)
    n_chunks = T // qc
    scale = 1.0 / math.sqrt(hd)
    k_mean = jnp.mean(kb.astype(jnp.float32), axis=2)
    bi = jnp.arange(B)[:, None, None, None]
    hi = jnp.arange(H)[None, :, None, None]
    blk_ids = jnp.arange(nb)
    offs = jnp.arange(MOBA_BLOCK)

    def chunk(args):
        q_c, c = args
        start = pos0 + c * qc
        qpos = start + jnp.arange(qc)
        own = start // MOBA_BLOCK
        qh = q_c.transpose(0, 2, 1, 3)
        gate = jnp.einsum('bhqd,bnhd->bhqn', qh.astype(jnp.float32), k_mean)
        gate = jnp.where(blk_ids < own, gate, -jnp.inf)
        _, idx = lax.top_k(gate, k_sel)
        valid = idx < own
        k_g = kb[bi, idx, :, hi, :]
        v_g = vb[bi, idx, :, hi, :]
        s_past = jnp.einsum('bhqd,bhqnkd->bhqnk', qh, k_g).astype(jnp.float32) * scale
        s_past = jnp.where(valid[..., None], s_past, -jnp.inf)
        k_own = lax.dynamic_index_in_dim(kb, own, axis=1, keepdims=False)
        v_own = lax.dynamic_index_in_dim(vb, own, axis=1, keepdims=False)
        s_own = jnp.einsum('bhqd,bkhd->bhqk', qh, k_own).astype(jnp.float32) * scale
        kpos = own * MOBA_BLOCK + offs
        s_own = jnp.where(kpos[None, None, None, :] <= qpos[None, None, :, None], s_own, -jnp.inf)
        s = jnp.concatenate([s_past.reshape(B, H, qc, k_sel * MOBA_BLOCK), s_own], axis=-1)
        p = jax.nn.softmax(s, axis=-1).astype(vb.dtype)
        p_past = p[..., :k_sel * MOBA_BLOCK].reshape(B, H, qc, k_sel, MOBA_BLOCK)
        p_own = p[..., k_sel * MOBA_BLOCK:]
        return (jnp.einsum('bhqnk,bhqnkd->bqhd', p_past, v_g)
                + jnp.einsum('bhqk,bkhd->bqhd', p_own, v_own))

    q_chunks = q.reshape(B, n_chunks, qc, H, hd).transpose(1, 0, 2, 3, 4)
    out = lax.map(chunk, (q_chunks, jnp.arange(n_chunks, dtype=jnp.int32)))
    return out.transpose(1, 0, 2, 3, 4).reshape(B, T, H, hd)


def fox_attend(q, k_all, v_all, r_all, pos0):
    B, T, H, hd = q.shape
    L = k_all.shape[1]
    qb = math.gcd(T, FOX_QBLOCK)
    nq = T // qb
    scale = 1.0 / math.sqrt(hd)
    kpos = jnp.arange(L)
    r_k = r_all.transpose(0, 2, 1)
    r_q = r_k[:, :, pos0:pos0 + T]

    def block(args):
        q_c, rq_c, c = args
        qpos = pos0 + c * qb + jnp.arange(qb)
        s = jnp.einsum('bqhd,bkhd->bhqk', q_c, k_all).astype(jnp.float32) * scale
        s = s + (r_k[:, :, None, :] - rq_c[..., None])
        s = jnp.where(kpos[None, None, None, :] <= qpos[None, None, :, None], s, -jnp.inf)
        p = jax.nn.softmax(s, axis=-1).astype(v_all.dtype)
        return jnp.einsum('bhqk,bkhd->bqhd', p, v_all)

    q_chunks = q.reshape(B, nq, qb, H, hd).transpose(1, 0, 2, 3, 4)
    rq_chunks = r_q.reshape(B, H, nq, qb).transpose(2, 0, 1, 3)
    out = lax.map(block, (q_chunks, rq_chunks, jnp.arange(nq, dtype=jnp.int32)))
    return out.transpose(1, 0, 2, 3, 4).reshape(B, T, H, hd)


def moba_mixer(xn, w_qkv, w_o, pos0, k_past=None, v_past=None):
    B, T, _ = xn.shape
    qkv = (xn @ w_qkv).reshape(B, T, 3, N_HEADS, HEAD_DIM)
    pos = pos0 + jnp.arange(T, dtype=jnp.int32)
    q = rope_partial(qkv[:, :, 0], pos)
    k = rope_partial(qkv[:, :, 1], pos)
    v = qkv[:, :, 2]
    L = pos0 + T
    nb = -(-L // MOBA_BLOCK)
    pad = jnp.zeros((B, nb * MOBA_BLOCK - L, N_HEADS, HEAD_DIM), k.dtype)
    k_parts = [k, pad] if k_past is None else [k_past, k, pad]
    v_parts = [v, pad] if v_past is None else [v_past, v, pad]
    kb = jnp.concatenate(k_parts, axis=1).reshape(B, nb, MOBA_BLOCK, N_HEADS, HEAD_DIM)
    vb = jnp.concatenate(v_parts, axis=1).reshape(B, nb, MOBA_BLOCK, N_HEADS, HEAD_DIM)
    o = moba_attend(q, kb, vb, pos0)
    return o.reshape(B, T, ATT_DIM) @ w_o, k, v


def fox_mixer(xn, w_in, b_f, w_o, pos0, k_past=None, v_past=None, lf_past=None):
    B, T, _ = xn.shape
    proj = xn @ w_in
    qkv = proj[..., :3 * ATT_DIM].reshape(B, T, 3, N_HEADS, HEAD_DIM)
    q, k, v = qkv[:, :, 0], qkv[:, :, 1], qkv[:, :, 2]
    lf = jax.nn.log_sigmoid((proj[..., 3 * ATT_DIM:] + b_f).astype(jnp.float32))
    if k_past is None:
        k_all, v_all, lf_all = k, v, lf
    else:
        k_all = jnp.concatenate([k_past, k], axis=1)
        v_all = jnp.concatenate([v_past, v], axis=1)
        lf_all = jnp.concatenate([lf_past.astype(jnp.float32), lf], axis=1)
    r_all = lax.cumsum(lf_all, axis=1, reverse=True) - lf_all
    o = fox_attend(q, k_all, v_all, r_all, pos0)
    return o.reshape(B, T, ATT_DIM) @ w_o, k, v, lf


def swiglu(xn, w_gu, w_down):
    g, u = jnp.split(xn @ w_gu, 2, axis=-1)
    return (jax.nn.silu(g) * u) @ w_down


def moe_swiglu(x, w_router, w_gu_e, w_down_e):
    B, T, D = x.shape
    N = B * T
    xt = x.reshape(N, D)
    logits = (xt @ w_router).astype(jnp.float32)
    top_val, top_idx = lax.top_k(logits, MOE_TOPK)
    gates = jax.nn.softmax(top_val, axis=-1)
    n_asg = N * MOE_TOPK
    blk = min(MOE_ROW_BLOCK, n_asg)
    n_rows = -(-(n_asg + N_EXPERTS * (blk - 1)) // blk) * blk
    flat_e = top_idx.reshape(-1).astype(jnp.int32)
    flat_tok = jnp.repeat(jnp.arange(N, dtype=jnp.int32), MOE_TOPK)
    flat_gate = gates.reshape(-1)
    order = jnp.argsort(flat_e)
    se, stok, sgate = flat_e[order], flat_tok[order], flat_gate[order]
    counts = jnp.bincount(flat_e, length=N_EXPERTS)
    start = jnp.cumsum(counts) - counts
    padded = (counts + blk - 1) // blk * blk
    pad_end = jnp.cumsum(padded)
    pad_start = pad_end - padded
    row = pad_start[se] + (jnp.arange(n_asg, dtype=jnp.int32) - start[se])
    row_tok = jnp.full((n_rows,), N, jnp.int32).at[row].set(stok)
    row_gate = jnp.zeros((n_rows,), jnp.float32).at[row].set(sgate)
    blk_start = jnp.arange(n_rows // blk, dtype=jnp.int32) * blk
    blk_e = jnp.minimum(jnp.searchsorted(pad_end, blk_start, side='right'), N_EXPERTS - 1)
    x_pad = jnp.concatenate([xt, jnp.zeros((1, D), xt.dtype)], axis=0)
    xs = x_pad[row_tok].reshape(n_rows // blk, blk, D)

    def expert_block(args):
        xb, e = args
        return swiglu(xb, w_gu_e[e], w_down_e[e])

    ys = lax.map(expert_block, (xs, blk_e)).reshape(n_rows, D)
    y = jax.ops.segment_sum(ys.astype(jnp.float32) * row_gate[:, None], row_tok,
                            num_segments=N + 1)[:N]
    return y.astype(x.dtype).reshape(B, T, D)


def setup_inputs(seed: int = 0) -> dict:
    key = jax.random.key(seed)
    ks = jax.random.split(key, 24)
    f32 = jnp.float32
    n_pages = PAST_LEN // PAGE_SIZE
    n_pool = (DEC_BATCH * n_pages * 5) // 4

    def normal(k, shape, scale=1.0):
        return jax.random.normal(k, shape, f32) * scale

    kv_a = (N_MOBA_LAYERS, n_pool, PAGE_SIZE, N_HEADS, HEAD_DIM)
    kv_b = (N_FOX_LAYERS, n_pool, PAGE_SIZE, N_HEADS, HEAD_DIM)
    page_table = jax.random.permutation(ks[7], n_pool)[:DEC_BATCH * n_pages]
    page_table = page_table.reshape(DEC_BATCH, n_pages).astype(jnp.int32)
    return {
        'x_prompt': normal(ks[0], (BATCH, SEQ, D_MODEL)),
        'x_sample': normal(ks[1], (DEC_BATCH, DEC_SEQ, D_MODEL)),
        'cache_k_moba': normal(ks[2], kv_a),
        'cache_v_moba': normal(ks[3], kv_a),
        'cache_k_fox': normal(ks[4], kv_b),
        'cache_v_fox': normal(ks[5], kv_b),
        'cache_lf_fox': jax.nn.log_sigmoid(FORGET_BIAS + normal(ks[6], (N_FOX_LAYERS, n_pool, PAGE_SIZE, N_HEADS))),
        'page_table': page_table,
        'norm_mixer': 1.0 + normal(ks[8], (DEPTH, D_MODEL), 0.02),
        'norm_ffn': 1.0 + normal(ks[9], (DEPTH, D_MODEL), 0.02),
        'moba_w_qkv': normal(ks[10], (N_MOBA_LAYERS, D_MODEL, 3 * ATT_DIM), D_MODEL ** -0.5),
        'moba_w_o': normal(ks[11], (N_MOBA_LAYERS, ATT_DIM, D_MODEL), ATT_DIM ** -0.5),
        'fox_w_in': normal(ks[12], (N_FOX_LAYERS, D_MODEL, 3 * ATT_DIM + N_HEADS), D_MODEL ** -0.5),
        'fox_b_f': FORGET_BIAS + normal(ks[13], (N_FOX_LAYERS, N_HEADS), 0.1),
        'fox_w_o': normal(ks[14], (N_FOX_LAYERS, ATT_DIM, D_MODEL), ATT_DIM ** -0.5),
        'ffn_w_gu': normal(ks[15], (N_MOBA_LAYERS, D_MODEL, 2 * D_FF), D_MODEL ** -0.5),
        'ffn_w_down': normal(ks[16], (N_MOBA_LAYERS, D_FF, D_MODEL), D_FF ** -0.5),
        'moe_w_router': normal(ks[17], (N_FOX_LAYERS, D_MODEL, N_EXPERTS), D_MODEL ** -0.5),
        'moe_w_gu': normal(ks[18], (N_FOX_LAYERS, N_EXPERTS, D_MODEL, 2 * D_FF_EXPERT), D_MODEL ** -0.5),
        'moe_w_down': normal(ks[19], (N_FOX_LAYERS, N_EXPERTS, D_FF_EXPERT, D_MODEL), D_FF_EXPERT ** -0.5),
        'norm_final': 1.0 + normal(ks[20], (D_MODEL,), 0.02),
    }


def reference(x_prompt, x_sample, cache_k_moba, cache_v_moba, cache_k_fox, cache_v_fox,
              cache_lf_fox, page_table, norm_mixer, norm_ffn, moba_w_qkv, moba_w_o,
              fox_w_in, fox_b_f, fox_w_o, ffn_w_gu, ffn_w_down, moe_w_router, moe_w_gu,
              moe_w_down, norm_final):
    xp, xs = x_prompt, x_sample
    km_p, vm_p, kf_p, vf_p, lf_p = [], [], [], [], []
    km_s, vm_s, kf_s, vf_s, lf_s = [], [], [], [], []
    for i in range(DEPTH):
        j = i // 2
        hp = rms_norm(xp, norm_mixer[i])
        hs = rms_norm(xs, norm_mixer[i])
        if i % 2 == 0:
            yp, kp, vp = moba_mixer(hp, moba_w_qkv[j], moba_w_o[j], 0)
            ys, k_new, v_new = moba_mixer(hs, moba_w_qkv[j], moba_w_o[j], PAST_LEN,
                                          gather_pages(cache_k_moba, j, page_table),
                                          gather_pages(cache_v_moba, j, page_table))
            km_p.append(kp); vm_p.append(vp); km_s.append(k_new); vm_s.append(v_new)
        else:
            yp, kp, vp, lfp = fox_mixer(hp, fox_w_in[j], fox_b_f[j], fox_w_o[j], 0)
            ys, k_new, v_new, lf_new = fox_mixer(hs, fox_w_in[j], fox_b_f[j], fox_w_o[j], PAST_LEN,
                                                 gather_pages(cache_k_fox, j, page_table),
                                                 gather_pages(cache_v_fox, j, page_table),
                                                 gather_pages(cache_lf_fox, j, page_table))
            kf_p.append(kp); vf_p.append(vp); lf_p.append(lfp)
            kf_s.append(k_new); vf_s.append(v_new); lf_s.append(lf_new)
        xp = xp + yp
        xs = xs + ys
        hp = rms_norm(xp, norm_ffn[i])
        hs = rms_norm(xs, norm_ffn[i])
        if i % 2 == 0:
            xp = xp + swiglu(hp, ffn_w_gu[j], ffn_w_down[j])
            xs = xs + swiglu(hs, ffn_w_gu[j], ffn_w_down[j])
        else:
            xp = xp + moe_swiglu(hp, moe_w_router[j], moe_w_gu[j], moe_w_down[j])
            xs = xs + moe_swiglu(hs, moe_w_router[j], moe_w_gu[j], moe_w_down[j])
    y_prompt = rms_norm(xp, norm_final)
    y_sample = rms_norm(xs, norm_final)
    k_moba_prompt = jnp.stack(km_p)
    v_moba_prompt = jnp.stack(vm_p)
    k_fox_prompt = jnp.stack(kf_p)
    v_fox_prompt = jnp.stack(vf_p)
    lf_fox_prompt = jnp.stack(lf_p)
    k_moba_sample = jnp.stack(km_s)
    v_moba_sample = jnp.stack(vm_s)
    k_fox_sample = jnp.stack(kf_s)
    v_fox_sample = jnp.stack(vf_s)
    lf_fox_sample = jnp.stack(lf_s)
    return (y_prompt, y_sample, k_moba_prompt, v_moba_prompt, k_fox_prompt, v_fox_prompt,
            lf_fox_prompt, k_moba_sample, v_moba_sample, k_fox_sample, v_fox_sample,
            lf_fox_sample)
```

```python
import functools
import math

import jax
import jax.numpy as jnp
from jax import lax
from jax.experimental import pallas as pl
from jax.experimental.pallas import tpu as pltpu

F32 = jnp.float32
BF16 = jnp.bfloat16
HIGHEST = lax.Precision.HIGHEST
NEG_INF = float("-inf")

N_HEADS = 8
HEAD_DIM = 128
ATT_DIM = N_HEADS * HEAD_DIM
ROPE_DIM = HEAD_DIM // 4
ROPE_THETA = 500000.0
MOBA_BLOCK = 256
MOBA_TOPK = 3
PAGE_SIZE = 128
MOE_TOPK = 2
MOE_ROW_BLOCK = 256
RMS_EPS = 1e-6
ATT_SCALE = 1.0 / math.sqrt(HEAD_DIM)

LANES = 128
SUBLANES = 8
V7X_VMEM_LIMIT_BYTES = 56 << 20
ROW_TILE = 512
FF_TILE = 256
ATT_TILE = 256
DEC_PAGES_PER_STEP = 8
PAGES_PER_BLOCK = MOBA_BLOCK // PAGE_SIZE

assert N_HEADS == SUBLANES and HEAD_DIM == LANES


def _cparams(*sem):
    return pltpu.CompilerParams(dimension_semantics=sem, vmem_limit_bytes=V7X_VMEM_LIMIT_BYTES)


def _row_tile(n):
    return ROW_TILE if n % ROW_TILE == 0 else n


def _rms_norm(x, g):
    return x * lax.rsqrt(jnp.mean(x * x, axis=-1, keepdims=True) + RMS_EPS) * g


def _log_sigmoid(z):
    return -(jnp.maximum(-z, 0.0) + jnp.log1p(jnp.exp(-jnp.abs(z))))


def _dot_nt(a, b, **kw):
    return lax.dot_general(a, b, (((1,), (1,)), ((), ())), preferred_element_type=F32, **kw)


def _head(h):
    return slice(h * HEAD_DIM, (h + 1) * HEAD_DIM)


def _in_proj_kernel(*refs, rope, gate, prompt, kmean):
    it = iter(refs)
    x_ref, g_ref, w_ref = next(it), next(it), next(it)
    if rope:
        cos_ref, sa_ref, sb_ref = next(it), next(it), next(it)
    if gate:
        wf_ref, bf_ref = next(it), next(it)
    q_ref, k_ref, v_ref = next(it), next(it), next(it)
    if prompt:
        kb_ref, vb_ref = next(it), next(it)
    if kmean:
        km_ref = next(it)
    tm = x_ref.shape[0]
    hn = _rms_norm(x_ref[...], g_ref[...]).astype(BF16)
    half = ROPE_DIM // 2
    for c in range(3):
        y = jnp.dot(hn, w_ref[:, c * ATT_DIM:(c + 1) * ATT_DIM], preferred_element_type=F32)
        for h in range(N_HEADS):
            yh = y[:, _head(h)]
            if rope and c < 2:
                yh = (yh * cos_ref[...] + pltpu.roll(yh, HEAD_DIM - half, 1) * sa_ref[...]
                      + pltpu.roll(yh, half, 1) * sb_ref[...])
            per_head_rows = pl.ds(h, tm, stride=N_HEADS)
            if c == 0:
                if prompt:
                    q_ref[:, _head(h)] = yh.astype(q_ref.dtype)
                else:
                    q_ref[per_head_rows, :] = yh
            else:
                (k_ref if c == 1 else v_ref)[per_head_rows, :] = yh
                if prompt:
                    (kb_ref if c == 1 else vb_ref)[:, _head(h)] = yh.astype(BF16)
                if kmean and c == 1:
                    for blk in range(tm // MOBA_BLOCK):
                        rows = yh[blk * MOBA_BLOCK:(blk + 1) * MOBA_BLOCK]
                        km_ref[blk, :, _head(h)] = jnp.mean(rows, axis=0, keepdims=True)
    if gate:
        lf_ref = next(it)
        z = jnp.dot(hn, wf_ref[...], preferred_element_type=F32) + bf_ref[...]
        lf_ref[...] = _log_sigmoid(z)


def _rope_tables(pos):
    half = ROPE_DIM // 2
    inv = jnp.power(jnp.float32(ROPE_THETA), -jnp.arange(half, dtype=F32) / half)
    ang = pos.astype(F32)[:, None] * inv[None, :]
    cos, sin = jnp.cos(ang), jnp.sin(ang)
    t = pos.shape[0]
    rest = HEAD_DIM - ROPE_DIM
    cos_t = jnp.concatenate([cos, cos, jnp.ones((t, rest), F32)], axis=1)
    sa_t = jnp.concatenate([-sin, jnp.zeros((t, half + rest), F32)], axis=1)
    sb_t = jnp.concatenate([jnp.zeros((t, half), F32), sin, jnp.zeros((t, rest), F32)], axis=1)
    return cos_t, sa_t, sb_t


def _in_proj(x2d, g, w_bf, *, prompt, rope_pos=None, kmean=False, gate_w=None, gate_b=None, q_dtype=F32):
    n, d = x2d.shape
    tm = _row_tile(n)
    rope, gate = rope_pos is not None, gate_w is not None
    assert not kmean or (prompt and tm % MOBA_BLOCK == 0)
    args = [x2d, g.reshape(1, d), w_bf]
    in_specs = [pl.BlockSpec((tm, d), lambda i: (i, 0)),
                pl.BlockSpec((1, d), lambda i: (0, 0)),
                pl.BlockSpec((d, 3 * ATT_DIM), lambda i: (0, 0))]
    if rope:
        nt = rope_pos.shape[0] // tm
        args += list(_rope_tables(rope_pos))
        in_specs += [pl.BlockSpec((tm, HEAD_DIM), lambda i: (i % nt, 0))] * 3
    if gate:
        args += [gate_w, gate_b]
        in_specs += [pl.BlockSpec((d, LANES), lambda i: (0, 0)), pl.BlockSpec((1, LANES), lambda i: (0, 0))]

    rows2d = (jax.ShapeDtypeStruct((n, ATT_DIM), F32), pl.BlockSpec((tm, ATT_DIM), lambda i: (i, 0)))
    head_rows = (jax.ShapeDtypeStruct((n * N_HEADS, HEAD_DIM), F32),
                 pl.BlockSpec((tm * N_HEADS, HEAD_DIM), lambda i: (i, 0)))
    names = ["q", "k", "v"]
    outs = [(jax.ShapeDtypeStruct((n, ATT_DIM), q_dtype), rows2d[1]) if prompt else head_rows, head_rows, head_rows]
    if prompt:
        names += ["kb", "vb"]
        outs += [(jax.ShapeDtypeStruct((n, ATT_DIM), BF16), rows2d[1])] * 2
    if kmean:
        names.append("km")
        nbt = tm // MOBA_BLOCK
        outs.append((jax.ShapeDtypeStruct((n // MOBA_BLOCK, 1, ATT_DIM), F32),
                     pl.BlockSpec((nbt, 1, ATT_DIM), lambda i: (i, 0, 0))))
    if gate:
        names.append("lf")
        outs.append((jax.ShapeDtypeStruct((n, LANES), F32), pl.BlockSpec((tm, LANES), lambda i: (i, 0))))
    res = pl.pallas_call(
        functools.partial(_in_proj_kernel, rope=rope, gate=gate, prompt=prompt, kmean=kmean),
        grid=(n // tm,), in_specs=in_specs,
        out_specs=[o[1] for o in outs], out_shape=[o[0] for o in outs],
        compiler_params=_cparams("parallel"), name="in_proj")(*args)
    return dict(zip(names, res))


def _out_proj_kernel(x_ref, o_ref, w_ref, y_ref):
    y_ref[...] = x_ref[...] + jnp.dot(o_ref[...].astype(BF16), w_ref[...], preferred_element_type=F32)


def _out_proj(x2d, o, w_bf):
    n, d = x2d.shape
    tm = _row_tile(n)
    return pl.pallas_call(
        _out_proj_kernel, grid=(n // tm,),
        in_specs=[pl.BlockSpec((tm, d), lambda i: (i, 0)),
                  pl.BlockSpec((tm, ATT_DIM), lambda i: (i, 0)),
                  pl.BlockSpec((ATT_DIM, d), lambda i: (0, 0))],
        out_specs=pl.BlockSpec((tm, d), lambda i: (i, 0)),
        out_shape=jax.ShapeDtypeStruct((n, d), F32),
        compiler_params=_cparams("parallel"), name="out_proj")(x2d, o, w_bf)


def _swiglu_into(act_ref, xb, wgu_ref, d_ff):
    for t in range(d_ff // FF_TILE):
        lo, hi = t * FF_TILE, (t + 1) * FF_TILE
        g = jnp.dot(xb, wgu_ref[:, lo:hi], preferred_element_type=F32)
        u = jnp.dot(xb, wgu_ref[:, d_ff + lo:d_ff + hi], preferred_element_type=F32)
        act_ref[:, lo:hi] = (g * jax.nn.sigmoid(g) * u).astype(BF16)


def _ffn_kernel(x_ref, g_ref, wgu_ref, wd_ref, gfin_ref, y_ref, act_ref, *, d_ff, final_norm):
    x = x_ref[...]
    hn = _rms_norm(x, g_ref[...]).astype(BF16)
    _swiglu_into(act_ref, hn, wgu_ref, d_ff)
    y = x + jnp.dot(act_ref[...], wd_ref[...], preferred_element_type=F32)
    if final_norm:
        y = _rms_norm(y, gfin_ref[...])
    y_ref[...] = y


def _dense_ffn(x2d, g, wgu_bf, wd_bf, g_final, final_norm):
    n, d = x2d.shape
    d_ff = wd_bf.shape[0]
    assert d_ff % FF_TILE == 0
    tm = _row_tile(n)
    return pl.pallas_call(
        functools.partial(_ffn_kernel, d_ff=d_ff, final_norm=final_norm),
        grid=(n // tm,),
        in_specs=[pl.BlockSpec((tm, d), lambda i: (i, 0)),
                  pl.BlockSpec((1, d), lambda i: (0, 0)),
                  pl.BlockSpec((d, 2 * d_ff), lambda i: (0, 0)),
                  pl.BlockSpec((d_ff, d), lambda i: (0, 0)),
                  pl.BlockSpec((1, d), lambda i: (0, 0))],
        out_specs=pl.BlockSpec((tm, d), lambda i: (i, 0)),
        out_shape=jax.ShapeDtypeStruct((n, d), F32),
        scratch_shapes=[pltpu.VMEM((tm, d_ff), BF16)],
        compiler_params=_cparams("parallel"), name="dense_ffn",
    )(x2d, g.reshape(1, d), wgu_bf, wd_bf, g_final.reshape(1, d))


def _router_kernel(x_ref, g_ref, wr_ref, hn_ref, route_ref, *, n_experts):
    hn = _rms_norm(x_ref[...], g_ref[...])
    hn_ref[...] = hn
    logits = jnp.dot(hn, wr_ref[...], preferred_element_type=F32, precision=HIGHEST)
    lane = lax.broadcasted_iota(jnp.int32, logits.shape, 1)
    logits = jnp.where(lane < n_experts, logits, NEG_INF)
    m1 = jnp.max(logits, axis=1, keepdims=True)
    i1 = jnp.min(jnp.where(logits == m1, lane, LANES), axis=1, keepdims=True)
    rest = jnp.where(lane == i1, NEG_INF, logits)
    m2 = jnp.max(rest, axis=1, keepdims=True)
    i2 = jnp.min(jnp.where(rest == m2, lane, LANES), axis=1, keepdims=True)
    e2 = jnp.exp(m2 - m1)
    den = 1.0 + e2
    route_ref[...] = jnp.where(lane == 0, i1.astype(F32),
                     jnp.where(lane == 1, i2.astype(F32),
                     jnp.where(lane == 2, 1.0 / den,
                     jnp.where(lane == 3, e2 / den, 0.0))))


def _router(x2d, g, wr_pad, n_experts):
    n, d = x2d.shape
    tm = _row_tile(n)
    return pl.pallas_call(
        functools.partial(_router_kernel, n_experts=n_experts), grid=(n // tm,),
        in_specs=[pl.BlockSpec((tm, d), lambda i: (i, 0)),
                  pl.BlockSpec((1, d), lambda i: (0, 0)),
                  pl.BlockSpec((d, LANES), lambda i: (0, 0))],
        out_specs=[pl.BlockSpec((tm, d), lambda i: (i, 0)),
                   pl.BlockSpec((tm, LANES), lambda i: (i, 0))],
        out_shape=[jax.ShapeDtypeStruct((n, d), F32), jax.ShapeDtypeStruct((n, LANES), F32)],
        compiler_params=_cparams("parallel"), name="moe_router")(x2d, g.reshape(1, d), wr_pad)


def _row_copy(src_hbm, src_row, dst, dst_row, sem):
    return pltpu.make_async_copy(src_hbm.at[pl.ds(src_row, 1)], dst.at[pl.ds(dst_row, 1)], sem)


def _gather_rows_kernel(idx_ref, src_hbm, dst_hbm, sem, *, rows):
    base = pl.program_id(0) * rows

    def start(r, c):
        _row_copy(src_hbm, idx_ref[0, 0, r], dst_hbm, base + r, sem).start()
        return c

    def wait(r, c):
        _row_copy(src_hbm, 0, dst_hbm, base + r, sem).wait()
        return c

    lax.fori_loop(0, rows, start, 0)
    lax.fori_loop(0, rows, wait, 0)


def _gather_rows(src, idx, rows):
    n_rows = idx.shape[0]
    d = src.shape[1]
    nb = n_rows // rows
    return pl.pallas_call(
        functools.partial(_gather_rows_kernel, rows=rows), grid=(nb,),
        in_specs=[pl.BlockSpec((1, 1, rows), lambda i: (i, 0, 0), memory_space=pltpu.SMEM),
                  pl.BlockSpec(memory_space=pl.ANY)],
        out_specs=pl.BlockSpec(memory_space=pl.ANY),
        out_shape=jax.ShapeDtypeStruct((n_rows, d), src.dtype),
        scratch_shapes=[pltpu.SemaphoreType.DMA(())],
        compiler_params=_cparams("arbitrary"), name="moe_gather",
    )(idx.reshape(nb, 1, rows), src)


def _expert_kernel(be_ref, x_ref, wgu_ref, wd_ref, y_ref, act_ref, *, d_ff):
    del be_ref
    xb = x_ref[...].astype(BF16)
    _swiglu_into(act_ref, xb, wgu_ref.at[0], d_ff)
    y_ref[...] = jnp.dot(act_ref[...], wd_ref[0], preferred_element_type=F32)


def _expert_ffn(xs, blk_e, wgu_bf, wd_bf, rows):
    n_rows, d = xs.shape
    d_ff = wd_bf.shape[1]
    assert d_ff % FF_TILE == 0
    grid_spec = pltpu.PrefetchScalarGridSpec(
        num_scalar_prefetch=1, grid=(n_rows // rows,),
        in_specs=[pl.BlockSpec((rows, d), lambda i, be: (i, 0)),
                  pl.BlockSpec((1, d, 2 * d_ff), lambda i, be: (be[i], 0, 0), pipeline_mode=pl.Buffered(1)),
                  pl.BlockSpec((1, d_ff, d), lambda i, be: (be[i], 0, 0), pipeline_mode=pl.Buffered(1))],
        out_specs=pl.BlockSpec((rows, d), lambda i, be: (i, 0)),
        scratch_shapes=[pltpu.VMEM((rows, d_ff), BF16)])
    return pl.pallas_call(
        functools.partial(_expert_kernel, d_ff=d_ff), grid_spec=grid_spec,
        out_shape=jax.ShapeDtypeStruct((n_rows, d), F32),
        compiler_params=_cparams("arbitrary"), name="moe_experts")(blk_e, xs, wgu_bf, wd_bf)


def _combine_kernel(rows_ref, x_ref, route_ref, gfin_ref, ys_hbm, y_ref, buf, sem, *, tm, final_norm):
    def start(r, c):
        for k in range(MOE_TOPK):
            _row_copy(ys_hbm, rows_ref[0, 0, MOE_TOPK * r + k], buf.at[k], r, sem).start()
        return c

    def wait(r, c):
        for k in range(MOE_TOPK):
            _row_copy(ys_hbm, 0, buf.at[k], r, sem).wait()
        return c

    lax.fori_loop(0, tm, start, 0)
    lax.fori_loop(0, tm, wait, 0)
    route = route_ref[...]
    y = x_ref[...] + (buf[0] * route[:, 2:3] + buf[1] * route[:, 3:4])
    if final_norm:
        y = _rms_norm(y, gfin_ref[...])
    y_ref[...] = y


def _moe_combine(x2d, route, rowpos, ys, g_final, final_norm):
    n, d = x2d.shape
    tm = min(_row_tile(n), MOE_ROW_BLOCK)
    return pl.pallas_call(
        functools.partial(_combine_kernel, tm=tm, final_norm=final_norm), grid=(n // tm,),
        in_specs=[pl.BlockSpec((1, 1, MOE_TOPK * tm), lambda i: (i, 0, 0), memory_space=pltpu.SMEM),
                  pl.BlockSpec((tm, d), lambda i: (i, 0)),
                  pl.BlockSpec((tm, LANES), lambda i: (i, 0)),
                  pl.BlockSpec((1, d), lambda i: (0, 0)),
                  pl.BlockSpec(memory_space=pl.ANY)],
        out_specs=pl.BlockSpec((tm, d), lambda i: (i, 0)),
        out_shape=jax.ShapeDtypeStruct((n, d), F32),
        scratch_shapes=[pltpu.VMEM((MOE_TOPK, tm, d), F32), pltpu.SemaphoreType.DMA(())],
        compiler_params=_cparams("arbitrary"), name="moe_combine",
    )(rowpos.reshape(n // tm, 1, MOE_TOPK * tm), x2d, route, g_final.reshape(1, d), ys)


def _moe_plan(top_idx, n_experts):
    n = top_idx.shape[0]
    n_asg = n * MOE_TOPK
    blk = min(MOE_ROW_BLOCK, n_asg)
    n_rows = -(-(n_asg + n_experts * (blk - 1)) // blk) * blk
    flat_e = top_idx.reshape(-1)
    onehot = (flat_e[:, None] == jnp.arange(n_experts, dtype=jnp.int32)[None, :]).astype(jnp.int32)
    csum = jnp.cumsum(onehot, axis=0)
    counts = csum[-1]
    rank = jnp.take_along_axis(csum, flat_e[:, None], axis=1)[:, 0] - 1
    padded = (counts + blk - 1) // blk * blk
    pad_end = jnp.cumsum(padded)
    pad_start = pad_end - padded
    row = (pad_start[flat_e] + rank).astype(jnp.int32)
    flat_tok = jnp.repeat(jnp.arange(n, dtype=jnp.int32), MOE_TOPK)
    row_tok = jnp.zeros((n_rows,), jnp.int32).at[row].set(flat_tok)
    blk_start = jnp.arange(n_rows // blk, dtype=jnp.int32) * blk
    blk_e = jnp.minimum(jnp.searchsorted(pad_end, blk_start, side="right"), n_experts - 1).astype(jnp.int32)
    return row.reshape(n, MOE_TOPK), row_tok, blk_e, blk


def _moe_ffn(x2d, g, wr_pad, wgu_bf, wd_bf, g_final, final_norm):
    n_experts = wgu_bf.shape[0]
    hn, route = _router(x2d, g, wr_pad, n_experts)
    top_idx = route[:, :MOE_TOPK].astype(jnp.int32)
    rowpos, row_tok, blk_e, blk = _moe_plan(top_idx, n_experts)
    xs = _gather_rows(hn, row_tok, blk)
    ys = _expert_ffn(xs, blk_e, wgu_bf, wd_bf, blk)
    return _moe_combine(x2d, route, rowpos, ys, g_final, final_norm)


def _flash_step(carry, s, v_blk):
    m, l, acc = carry
    m_new = jnp.maximum(m, jnp.max(s, axis=1, keepdims=True))
    alpha = jnp.exp(m - m_new)
    p = jnp.exp(s - m_new)
    l = alpha * l + jnp.sum(p, axis=1, keepdims=True)
    acc = alpha * acc + jnp.dot(p.astype(BF16), v_blk, preferred_element_type=F32)
    return m_new, l, acc


def _flash_first(s, v_blk):
    m = jnp.max(s, axis=1, keepdims=True)
    p = jnp.exp(s - m)
    return m, jnp.sum(p, axis=1, keepdims=True), jnp.dot(p.astype(BF16), v_blk, preferred_element_type=F32)


def _causal_mask(s):
    row = lax.broadcasted_iota(jnp.int32, s.shape, 0)
    col = lax.broadcasted_iota(jnp.int32, s.shape, 1)
    return jnp.where(col <= row, s, NEG_INF)


def _moba_attn_kernel(q_ref, kb_ref, vb_ref, km_ref, o_ref, kmp_ref, *, nb):
    qi = pl.program_id(2)

    @pl.when(qi == 0)
    def _():
        kmp_ref[...] = jnp.zeros(kmp_ref.shape, F32)
        kmp_ref[0:nb, :] = km_ref[...].reshape(nb, HEAD_DIM)

    q = q_ref[...]
    gate = _dot_nt(q, kmp_ref[...], precision=HIGHEST)
    lane = lax.broadcasted_iota(jnp.int32, gate.shape, 1)
    g = jnp.where(lane < qi, gate, NEG_INF)
    sel = jnp.zeros(gate.shape, F32)
    for _ in range(min(MOBA_TOPK, nb)):
        m = jnp.max(g, axis=1, keepdims=True)
        idx = jnp.min(jnp.where(g == m, lane, LANES), axis=1, keepdims=True)
        pick = lane == idx
        sel = jnp.where(pick, 1.0, sel)
        g = jnp.where(pick, NEG_INF, g)
    sel = jnp.where(lane < qi, sel, 0.0)

    qb = q.astype(BF16)

    def kv(n):
        start = pl.multiple_of(n * ATT_TILE, ATT_TILE)
        return kb_ref[pl.ds(start, ATT_TILE), :], vb_ref[pl.ds(start, ATT_TILE), :]

    k_own, v_own = kv(qi)
    carry = _flash_first(_causal_mask(_dot_nt(qb, k_own) * ATT_SCALE), v_own)

    def past(n, carry):
        k_n, v_n = kv(n)
        chosen = jnp.sum(jnp.where(lane == n, sel, 0.0), axis=1, keepdims=True) > 0.0
        s = jnp.where(chosen, _dot_nt(qb, k_n) * ATT_SCALE, NEG_INF)
        return _flash_step(carry, s, v_n)

    m, l, acc = lax.fori_loop(0, qi, past, carry)
    o_ref[...] = (acc / l).astype(o_ref.dtype)


def _moba_prompt_attn(q, kb, vb, km, batch, seq):
    assert seq % MOBA_BLOCK == 0 and ATT_TILE == MOBA_BLOCK
    nb = seq // MOBA_BLOCK
    assert nb <= LANES
    q_spec = pl.BlockSpec((ATT_TILE, HEAD_DIM), lambda b, h, i: (b * nb + i, h))
    kv_spec = pl.BlockSpec((seq, HEAD_DIM), lambda b, h, i: (b, h))
    return pl.pallas_call(
        functools.partial(_moba_attn_kernel, nb=nb), grid=(batch, N_HEADS, nb),
        in_specs=[q_spec, kv_spec, kv_spec, pl.BlockSpec((nb, 1, HEAD_DIM), lambda b, h, i: (b, 0, h))],
        out_specs=q_spec,
        out_shape=jax.ShapeDtypeStruct(q.shape, BF16),
        scratch_shapes=[pltpu.VMEM((LANES, HEAD_DIM), F32)],
        compiler_params=_cparams("parallel", "parallel", "arbitrary"), name="moba_prompt_attn")(q, kb, vb, km)


def _fox_attn_kernel(q_ref, kb_ref, vb_ref, lf_ref, o_ref, nc_ref, *, n_chunks):
    qi = pl.program_id(2)

    @pl.when(qi == 0)
    def _():
        x = lf_ref[0, 0]
        lane = lax.broadcasted_iota(jnp.int32, x.shape, 1)
        d = 1
        while d < LANES:
            x = x + jnp.where(lane >= d, pltpu.roll(x, d, 1), 0.0)
            d *= 2
        run = jnp.zeros((1, LANES), F32)
        for r in range(n_chunks):
            row = x[r:r + 1, :] + run
            nc_ref[:, r * LANES:(r + 1) * LANES] = -row
            run = run + x[r:r + 1, LANES - 1:LANES]

    qb = q_ref[...]

    def scores(n):
        start = pl.multiple_of(n * ATT_TILE, ATT_TILE)
        s = _dot_nt(qb, kb_ref[pl.ds(start, ATT_TILE), :]) * ATT_SCALE + nc_ref[:, pl.ds(start, ATT_TILE)]
        return s, vb_ref[pl.ds(start, ATT_TILE), :]

    s_own, v_own = scores(qi)
    carry = _flash_first(_causal_mask(s_own), v_own)

    def past(n, carry):
        s, v_n = scores(n)
        return _flash_step(carry, s, v_n)

    m, l, acc = lax.fori_loop(0, qi, past, carry)
    o_ref[...] = (acc / l).astype(o_ref.dtype)


def _fox_prompt_attn(q_bf, kb, vb, lf, batch, seq):
    assert seq % ATT_TILE == 0
    nq = seq // ATT_TILE
    n_chunks = seq // LANES
    lf_t = lf[:, :N_HEADS].reshape(batch, seq, N_HEADS).transpose(0, 2, 1).reshape(batch, N_HEADS, n_chunks, LANES)
    q_spec = pl.BlockSpec((ATT_TILE, HEAD_DIM), lambda b, h, i: (b * nq + i, h))
    kv_spec = pl.BlockSpec((seq, HEAD_DIM), lambda b, h, i: (b, h))
    return pl.pallas_call(
        functools.partial(_fox_attn_kernel, n_chunks=n_chunks), grid=(batch, N_HEADS, nq),
        in_specs=[q_spec, kv_spec, kv_spec, pl.BlockSpec((1, 1, n_chunks, LANES), lambda b, h, i: (b, h, 0, 0))],
        out_specs=q_spec,
        out_shape=jax.ShapeDtypeStruct(q_bf.shape, BF16),
        scratch_shapes=[pltpu.VMEM((1, seq), F32)],
        compiler_params=_cparams("parallel", "parallel", "arbitrary"), name="fox_prompt_attn")(q_bf, kb, vb, lf_t)


def _page_spec(tail, layer, p, per_step):
    zeros = (0,) * len(tail)
    return pl.BlockSpec((1, 1) + tail, lambda b, j, pt: (layer, pt[b, j * per_step + p]) + zeros)


def _tok_spec():
    return pl.BlockSpec((1, N_HEADS, HEAD_DIM), lambda b, *_: (b, 0, 0))


def _moba_dec_stats_kernel(pt_ref, q_ref, *refs, per_step, n_blk):
    del pt_ref
    pages, sel_ref, gate_ref = refs[:per_step], refs[per_step], refs[per_step + 1]
    j = pl.program_id(1)
    lane = lax.broadcasted_iota(jnp.int32, (N_HEADS, LANES), 1)

    @pl.when(j == 0)
    def _():
        gate_ref[...] = jnp.full(gate_ref.shape, NEG_INF, F32)

    q = q_ref[0]
    blk_per_step = per_step // PAGES_PER_BLOCK
    for t in range(blk_per_step):
        ksum = jnp.sum(pages[PAGES_PER_BLOCK * t][0, 0], axis=0)
        for u in range(1, PAGES_PER_BLOCK):
            ksum = ksum + jnp.sum(pages[PAGES_PER_BLOCK * t + u][0, 0], axis=0)
        g = jnp.sum(ksum * q, axis=1, keepdims=True) * (1.0 / MOBA_BLOCK)
        gate_ref[...] = jnp.where(lane == j * blk_per_step + t, g, gate_ref[...])

    @pl.when(j == pl.num_programs(1) - 1)
    def _():
        g = gate_ref[...]
        out = jnp.zeros((N_HEADS, LANES), jnp.int32)
        for r in range(MOBA_TOPK):
            m = jnp.max(g, axis=1, keepdims=True)
            idx = jnp.min(jnp.where(g == m, lane, LANES), axis=1, keepdims=True)
            out = jnp.where(lane == r, idx, out)
            g = jnp.where(lane == idx, NEG_INF, g)
        sel_ref[0] = out


def _moba_dec_select(q3, cache_k, page_table, layer):
    dec_b, n_pages = page_table.shape
    per_step = math.gcd(n_pages, DEC_PAGES_PER_STEP)
    assert per_step % PAGES_PER_BLOCK == 0
    n_blk = n_pages // PAGES_PER_BLOCK
    assert MOBA_TOPK <= n_blk <= LANES
    grid_spec = pltpu.PrefetchScalarGridSpec(
        num_scalar_prefetch=1, grid=(dec_b, n_pages // per_step),
        in_specs=[_tok_spec()]
        + [_page_spec((PAGE_SIZE, N_HEADS, HEAD_DIM), layer, p, per_step) for p in range(per_step)],
        out_specs=pl.BlockSpec((1, N_HEADS, LANES), lambda b, j, pt: (b, 0, 0)),
        scratch_shapes=[pltpu.VMEM((N_HEADS, LANES), F32)])
    return pl.pallas_call(
        functools.partial(_moba_dec_stats_kernel, per_step=per_step, n_blk=n_blk), grid_spec=grid_spec,
        out_shape=jax.ShapeDtypeStruct((dec_b, N_HEADS, LANES), jnp.int32),
        compiler_params=_cparams("parallel", "arbitrary"), name="moba_dec_select",
    )(page_table, q3, *([cache_k] * per_step))


def _moba_dec_attn_kernel(pg_ref, q_ref, kn_ref, vn_ref, ck_hbm, cv_hbm, o_ref, kbuf, vbuf, sems, *, layer, n_sel):
    b = pl.program_id(0)

    def copies(h, t):
        page = pg_ref[(b * N_HEADS + h) * n_sel + t]
        rows = pl.ds(t * PAGE_SIZE, PAGE_SIZE)
        return (pltpu.make_async_copy(ck_hbm.at[layer, page, :, h, :], kbuf.at[h, rows, :], sems.at[0]),
                pltpu.make_async_copy(cv_hbm.at[layer, page, :, h, :], vbuf.at[h, rows, :], sems.at[1]))

    for h in range(N_HEADS):
        for t in range(n_sel):
            for cp in copies(h, t):
                cp.start()
    for h in range(N_HEADS):
        for t in range(n_sel):
            for cp in copies(h, t):
                cp.wait()

    for h in range(N_HEADS):
        q = q_ref[0, h:h + 1, :]
        q8 = jnp.broadcast_to(q, (SUBLANES, HEAD_DIM)).astype(BF16)
        s = _dot_nt(q8, kbuf[h].astype(BF16))[0:1] * ATT_SCALE
        s_new = jnp.sum(q * kn_ref[0, h:h + 1, :], axis=1, keepdims=True) * ATT_SCALE
        m = jnp.maximum(jnp.max(s, axis=1, keepdims=True), s_new)
        p = jnp.exp(s - m)
        p_new = jnp.exp(s_new - m)
        l = jnp.sum(p, axis=1, keepdims=True) + p_new
        p8 = jnp.broadcast_to(p, (SUBLANES, p.shape[1])).astype(BF16)
        pv = jnp.dot(p8, vbuf[h].astype(BF16), preferred_element_type=F32)[0:1]
        o_ref[0, h:h + 1, :] = (pv + p_new * vn_ref[0, h:h + 1, :]) / l


def _moba_dec_attn(q3, k_new3, v_new3, cache_k, cache_v, sel_pages, layer):
    dec_b = q3.shape[0]
    n_sel = MOBA_TOPK * PAGES_PER_BLOCK
    grid_spec = pltpu.PrefetchScalarGridSpec(
        num_scalar_prefetch=1, grid=(dec_b,),
        in_specs=[_tok_spec(), _tok_spec(), _tok_spec(),
                  pl.BlockSpec(memory_space=pl.ANY), pl.BlockSpec(memory_space=pl.ANY)],
        out_specs=_tok_spec(),
        scratch_shapes=[pltpu.VMEM((N_HEADS, n_sel * PAGE_SIZE, HEAD_DIM), F32),
                        pltpu.VMEM((N_HEADS, n_sel * PAGE_SIZE, HEAD_DIM), F32),
                        pltpu.SemaphoreType.DMA((2,))])
    return pl.pallas_call(
        functools.partial(_moba_dec_attn_kernel, layer=layer, n_sel=n_sel), grid_spec=grid_spec,
        out_shape=jax.ShapeDtypeStruct((dec_b, N_HEADS, HEAD_DIM), F32),
        compiler_params=_cparams("arbitrary"), name="moba_dec_attn",
    )(sel_pages, q3, k_new3, v_new3, cache_k, cache_v)


def _fox_dec_kernel(pt_ref, q_ref, kn_ref, vn_ref, lfn_ref, *refs, per_step):
    del pt_ref
    k_pages, v_pages, lf_pages = refs[:per_step], refs[per_step:2 * per_step], refs[2 * per_step:3 * per_step]
    o_ref, m_ref, l_ref, acc_ref, run_ref = refs[3 * per_step:]
    j = pl.program_id(1)
    flat = PAGE_SIZE * N_HEADS

    @pl.when(j == 0)
    def _():
        m_ref[...] = jnp.full(m_ref.shape, NEG_INF, F32)
        l_ref[...] = jnp.zeros(l_ref.shape, F32)
        acc_ref[...] = jnp.zeros(acc_ref.shape, F32)
        run_ref[...] = jnp.zeros(run_ref.shape, F32)

    q = q_ref[0]
    qb = q.astype(BF16)
    lane = lax.broadcasted_iota(jnp.int32, (SUBLANES, LANES), 1)
    sub = lax.broadcasted_iota(jnp.int32, (SUBLANES, LANES), 0)
    col_head = lax.broadcasted_iota(jnp.int32, (N_HEADS, flat), 1) & (N_HEADS - 1)
    own_head = col_head == lax.broadcasted_iota(jnp.int32, (N_HEADS, flat), 0)

    def update(s, pv_of):
        m_old = m_ref[...]
        m_new = jnp.maximum(m_old, jnp.max(s, axis=1, keepdims=True))
        alpha = jnp.exp(m_old - m_new)
        p = jnp.exp(s - m_new)
        l_ref[...] = alpha * l_ref[...] + jnp.sum(p, axis=1, keepdims=True)
        acc_ref[...] = alpha * acc_ref[...] + pv_of(p)
        m_ref[...] = m_new

    for t in range(per_step):
        x = lf_pages[t][0, 0]
        tot = x
        d = N_HEADS
        while d < LANES:
            x = x + jnp.where(lane >= d, pltpu.roll(x, d, 1), 0.0)
            tot = tot + pltpu.roll(tot, d, 1)
            d *= 2
        inc = tot
        d = 1
        while d < SUBLANES:
            inc = inc + jnp.where(sub >= d, pltpu.roll(inc, d, 0), 0.0)
            d *= 2
        run = run_ref[...]
        c = x + (inc - tot) + run
        run_ref[...] = run + inc[SUBLANES - 1:SUBLANES, :]
        bias = jnp.concatenate([-c[a:a + 1, :] for a in range(SUBLANES)], axis=1)

        k2 = k_pages[t][0, 0].reshape(flat, HEAD_DIM).astype(BF16)
        v2 = v_pages[t][0, 0].reshape(flat, HEAD_DIM).astype(BF16)
        s = jnp.where(own_head, _dot_nt(qb, k2) * ATT_SCALE + bias, NEG_INF)
        update(s, lambda p: jnp.dot(p.astype(BF16), v2, preferred_element_type=F32))

    @pl.when(j == pl.num_programs(1) - 1)
    def _():
        c_new = run_ref[...] + lfn_ref[0]
        c_col = jnp.sum(jnp.where(lane == sub, c_new, 0.0), axis=1, keepdims=True)
        s_new = jnp.sum(q * kn_ref[0], axis=1, keepdims=True) * ATT_SCALE - c_col
        update(s_new, lambda p: p * vn_ref[0])
        o_ref[0] = acc_ref[...] / l_ref[...]


def _fox_dec_attn(q3, k_new3, v_new3, lf_new, cache_k, cache_v, cache_lf, page_table, layer):
    dec_b, n_pages = page_table.shape
    per_step = math.gcd(n_pages, DEC_PAGES_PER_STEP // 2)
    lf_flat = cache_lf.reshape(cache_lf.shape[0], cache_lf.shape[1], PAGE_SIZE * N_HEADS // LANES, LANES)
    kv_tail = (PAGE_SIZE, N_HEADS, HEAD_DIM)
    grid_spec = pltpu.PrefetchScalarGridSpec(
        num_scalar_prefetch=1, grid=(dec_b, n_pages // per_step),
        in_specs=[_tok_spec(), _tok_spec(), _tok_spec(), pl.BlockSpec((1, 1, LANES), lambda b, j, pt: (b, 0, 0))]
        + [_page_spec(kv_tail, layer, p, per_step) for p in range(per_step)] * 2
        + [_page_spec(lf_flat.shape[2:], layer, p, per_step) for p in range(per_step)],
        out_specs=_tok_spec(),
        scratch_shapes=[pltpu.VMEM((N_HEADS, 1), F32), pltpu.VMEM((N_HEADS, 1), F32),
                        pltpu.VMEM((N_HEADS, HEAD_DIM), F32), pltpu.VMEM((SUBLANES, LANES), F32)])
    return pl.pallas_call(
        functools.partial(_fox_dec_kernel, per_step=per_step), grid_spec=grid_spec,
        out_shape=jax.ShapeDtypeStruct((dec_b, N_HEADS, HEAD_DIM), F32),
        compiler_params=_cparams("parallel", "arbitrary"), name="fox_dec_attn",
    )(page_table, q3, k_new3, v_new3, lf_new.reshape(dec_b, 1, LANES),
      *([cache_k] * per_step), *([cache_v] * per_step), *([lf_flat] * per_step))


def _pad_lanes(w):
    return jnp.pad(w, ((0, 0), (0, LANES - w.shape[1])))


def kernel(x_prompt, x_sample, cache_k_moba, cache_v_moba, cache_k_fox, cache_v_fox, cache_lf_fox, page_table, norm_mixer, norm_ffn, moba_w_qkv, moba_w_o, fox_w_in, fox_b_f, fox_w_o, ffn_w_gu, ffn_w_down, moe_w_router, moe_w_gu, moe_w_down, norm_final):
    batch, seq, d = x_prompt.shape
    dec_b, dec_seq, _ = x_sample.shape
    depth = norm_mixer.shape[0]
    past_len = page_table.shape[1] * PAGE_SIZE
    assert d == ATT_DIM and dec_seq == 1 and past_len % MOBA_BLOCK == 0
    assert cache_k_moba.shape[2:] == (PAGE_SIZE, N_HEADS, HEAD_DIM)

    xp = x_prompt.reshape(batch * seq, d)
    xs = x_sample.reshape(dec_b, d)
    pos_p = jnp.arange(seq, dtype=jnp.int32)
    pos_s = jnp.full((dec_b,), past_len, jnp.int32)
    outs = {name: [] for name in ("km_p", "vm_p", "kf_p", "vf_p", "lf_p", "km_s", "vm_s", "kf_s", "vf_s", "lf_s")}
    heads3 = lambda a: a.reshape(dec_b, N_HEADS, HEAD_DIM)

    for i in range(depth):
        j = i // 2
        last = i == depth - 1
        if i % 2 == 0:
            w_qkv = moba_w_qkv[j].astype(BF16)
            w_o = moba_w_o[j].astype(BF16)
            pp = _in_proj(xp, norm_mixer[i], w_qkv, prompt=True, rope_pos=pos_p, kmean=True)
            ps = _in_proj(xs, norm_mixer[i], w_qkv, prompt=False, rope_pos=pos_s)
            op = _moba_prompt_attn(pp["q"], pp["kb"], pp["vb"], pp["km"], batch, seq)
            q3 = heads3(ps["q"])
            sel = _moba_dec_select(q3, cache_k_moba, page_table, j)
            page_slot = (sel[:, :, :MOBA_TOPK, None] * PAGES_PER_BLOCK
                         + jnp.arange(PAGES_PER_BLOCK, dtype=jnp.int32)).reshape(dec_b, -1)
            sel_pages = jnp.take_along_axis(page_table, page_slot, axis=1).reshape(-1)
            os_ = _moba_dec_attn(q3, heads3(ps["k"]), heads3(ps["v"]), cache_k_moba, cache_v_moba, sel_pages, j)
            xp = _out_proj(xp, op, w_o)
            xs = _out_proj(xs, os_.reshape(dec_b, ATT_DIM), w_o)
            outs["km_p"].append(pp["k"]); outs["vm_p"].append(pp["v"])
            outs["km_s"].append(ps["k"]); outs["vm_s"].append(ps["v"])
            wgu, wd = ffn_w_gu[j].astype(BF16), ffn_w_down[j].astype(BF16)
            xp = _dense_ffn(xp, norm_ffn[i], wgu, wd, norm_final, last)
            xs = _dense_ffn(xs, norm_ffn[i], wgu, wd, norm_final, last)
        else:
            w_in = fox_w_in[j]
            w_qkv = w_in[:, :3 * ATT_DIM].astype(BF16)
            w_f = _pad_lanes(w_in[:, 3 * ATT_DIM:]).astype(BF16)
            b_f = _pad_lanes(fox_b_f[j].reshape(1, N_HEADS))
            w_o = fox_w_o[j].astype(BF16)
            pp = _in_proj(xp, norm_mixer[i], w_qkv, prompt=True, gate_w=w_f, gate_b=b_f, q_dtype=BF16)
            ps = _in_proj(xs, norm_mixer[i], w_qkv, prompt=False, gate_w=w_f, gate_b=b_f)
            op = _fox_prompt_attn(pp["q"], pp["kb"], pp["vb"], pp["lf"], batch, seq)
            os_ = _fox_dec_attn(heads3(ps["q"]), heads3(ps["k"]), heads3(ps["v"]), ps["lf"],
                                cache_k_fox, cache_v_fox, cache_lf_fox, page_table, j)
            xp = _out_proj(xp, op, w_o)
            xs = _out_proj(xs, os_.reshape(dec_b, ATT_DIM), w_o)
            outs["kf_p"].append(pp["k"]); outs["vf_p"].append(pp["v"]); outs["lf_p"].append(pp["lf"][:, :N_HEADS])
            outs["kf_s"].append(ps["k"]); outs["vf_s"].append(ps["v"]); outs["lf_s"].append(ps["lf"][:, :N_HEADS])
            wr = _pad_lanes(moe_w_router[j])
            wgu, wd = moe_w_gu[j].astype(BF16), moe_w_down[j].astype(BF16)
            xp = _moe_ffn(xp, norm_ffn[i], wr, wgu, wd, norm_final, last)
            xs = _moe_ffn(xs, norm_ffn[i], wr, wgu, wd, norm_final, last)

    def rows(name, b, t, tail):
        return jnp.stack(outs[name]).reshape((len(outs[name]), b, t) + tail)

    kv_tail = (N_HEADS, HEAD_DIM)
    return (xp.reshape(batch, seq, d), xs.reshape(dec_b, dec_seq, d),
            rows("km_p", batch, seq, kv_tail), rows("vm_p", batch, seq, kv_tail),
            rows("kf_p", batch, seq, kv_tail), rows("vf_p", batch, seq, kv_tail),
            rows("lf_p", batch, seq, (N_HEADS,)),
            rows("km_s", dec_b, dec_seq, kv_tail), rows("vm_s", dec_b, dec_seq, kv_tail),
            rows("kf_s", dec_b, dec_seq, kv_tail), rows("vf_s", dec_b, dec_seq, kv_tail),
            rows("lf_s", dec_b, dec_seq, (N_HEADS,)))
```

```python
import functools
import math

import jax
import jax.numpy as jnp
from jax import lax
from jax.experimental import pallas as pl
from jax.experimental.pallas import tpu as pltpu

F32 = jnp.float32
BF16 = jnp.bfloat16
HIGHEST = lax.Precision.HIGHEST
NEG_INF = float("-inf")

N_HEADS = 8
HEAD_DIM = 128
ATT_DIM = N_HEADS * HEAD_DIM
ROPE_DIM = HEAD_DIM // 4
ROPE_THETA = 500000.0
MOBA_BLOCK = 256
MOBA_TOPK = 3
PAGE_SIZE = 128
MOE_TOPK = 2
MOE_ROW_BLOCK = 256
RMS_EPS = 1e-6
ATT_SCALE = 1.0 / math.sqrt(HEAD_DIM)

LANES = 128
SUBLANES = 8
V7X_VMEM_LIMIT_BYTES = 56 << 20
ROW_TILE = 512
FF_TILE = 256
ATT_TILE = 256
DEC_PAGES_PER_STEP = 8
PAGES_PER_BLOCK = MOBA_BLOCK // PAGE_SIZE

assert N_HEADS == SUBLANES and HEAD_DIM == LANES


def _cparams(*sem):
    return pltpu.CompilerParams(dimension_semantics=sem, vmem_limit_bytes=V7X_VMEM_LIMIT_BYTES)


def _row_tile(n):
    return ROW_TILE if n % ROW_TILE == 0 else n


def _rms_norm(x, g):
    return x * lax.rsqrt(jnp.mean(x * x, axis=-1, keepdims=True) + RMS_EPS) * g


def _log_sigmoid(z):
    return -(jnp.maximum(-z, 0.0) + jnp.log1p(jnp.exp(-jnp.abs(z))))


def _dot_nt(a, b, **kw):
    return lax.dot_general(a, b, (((1,), (1,)), ((), ())), preferred_element_type=F32, **kw)


def _head(h):
    return slice(h * HEAD_DIM, (h + 1) * HEAD_DIM)


def _in_proj_kernel(*refs, rope, gate, prompt, kmean):
    it = iter(refs)
    x_ref, g_ref, w_ref = next(it), next(it), next(it)
    if rope:
        cos_ref, sa_ref, sb_ref = next(it), next(it), next(it)
    if gate:
        wf_ref, bf_ref = next(it), next(it)
    q_ref, k_ref, v_ref = next(it), next(it), next(it)
    if prompt:
        kb_ref, vb_ref = next(it), next(it)
    if kmean:
        km_ref = next(it)
    tm = x_ref.shape[0]
    hn = _rms_norm(x_ref[...], g_ref[...]).astype(BF16)
    half = ROPE_DIM // 2
    for c in range(3):
        y = jnp.dot(hn, w_ref[:, c * ATT_DIM:(c + 1) * ATT_DIM], preferred_element_type=F32)
        for h in range(N_HEADS):
            yh = y[:, _head(h)]
            if rope and c < 2:
                yh = (yh * cos_ref[...] + pltpu.roll(yh, HEAD_DIM - half, 1) * sa_ref[...]
                      + pltpu.roll(yh, half, 1) * sb_ref[...])
            per_head_rows = pl.ds(h, tm, stride=N_HEADS)
            if c == 0:
                if prompt:
                    q_ref[:, _head(h)] = yh.astype(q_ref.dtype)
                else:
                    q_ref[per_head_rows, :] = yh
            else:
                (k_ref if c == 1 else v_ref)[per_head_rows, :] = yh
                if prompt:
                    (kb_ref if c == 1 else vb_ref)[:, _head(h)] = yh.astype(BF16)
                if kmean and c == 1:
                    for blk in range(tm // MOBA_BLOCK):
                        rows = yh[blk * MOBA_BLOCK:(blk + 1) * MOBA_BLOCK]
                        km_ref[blk, :, _head(h)] = jnp.mean(rows, axis=0, keepdims=True)
    if gate:
        lf_ref = next(it)
        z = jnp.dot(hn, wf_ref[...], preferred_element_type=F32) + bf_ref[...]
        lf_ref[...] = _log_sigmoid(z)


def _rope_tables(pos):
    half = ROPE_DIM // 2
    inv = jnp.power(jnp.float32(ROPE_THETA), -jnp.arange(half, dtype=F32) / half)
    ang = pos.astype(F32)[:, None] * inv[None, :]
    cos, sin = jnp.cos(ang), jnp.sin(ang)
    t = pos.shape[0]
    rest = HEAD_DIM - ROPE_DIM
    cos_t = jnp.concatenate([cos, cos, jnp.ones((t, rest), F32)], axis=1)
    sa_t = jnp.concatenate([-sin, jnp.zeros((t, half + rest), F32)], axis=1)
    sb_t = jnp.concatenate([jnp.zeros((t, half), F32), sin, jnp.zeros((t, rest), F32)], axis=1)
    return cos_t, sa_t, sb_t


def _in_proj(x2d, g, w_bf, *, prompt, rope_pos=None, kmean=False, gate_w=None, gate_b=None, q_dtype=F32):
    n, d = x2d.shape
    tm = _row_tile(n)
    rope, gate = rope_pos is not None, gate_w is not None
    assert not kmean or (prompt and tm % MOBA_BLOCK == 0)
    args = [x2d, g.reshape(1, d), w_bf]
    in_specs = [pl.BlockSpec((tm, d), lambda i: (i, 0)),
                pl.BlockSpec((1, d), lambda i: (0, 0)),
                pl.BlockSpec((d, 3 * ATT_DIM), lambda i: (0, 0))]
    if rope:
        nt = rope_pos.shape[0] // tm
        args += list(_rope_tables(rope_pos))
        in_specs += [pl.BlockSpec((tm, HEAD_DIM), lambda i: (i % nt, 0))] * 3
    if gate:
        args += [gate_w, gate_b]
        in_specs += [pl.BlockSpec((d, LANES), lambda i: (0, 0)), pl.BlockSpec((1, LANES), lambda i: (0, 0))]

    rows2d = (jax.ShapeDtypeStruct((n, ATT_DIM), F32), pl.BlockSpec((tm, ATT_DIM), lambda i: (i, 0)))
    head_rows = (jax.ShapeDtypeStruct((n * N_HEADS, HEAD_DIM), F32),
                 pl.BlockSpec((tm * N_HEADS, HEAD_DIM), lambda i: (i, 0)))
    names = ["q", "k", "v"]
    outs = [(jax.ShapeDtypeStruct((n, ATT_DIM), q_dtype), rows2d[1]) if prompt else head_rows, head_rows, head_rows]
    if prompt:
        names += ["kb", "vb"]
        outs += [(jax.ShapeDtypeStruct((n, ATT_DIM), BF16), rows2d[1])] * 2
    if kmean:
        names.append("km")
        nbt = tm // MOBA_BLOCK
        outs.append((jax.ShapeDtypeStruct((n // MOBA_BLOCK, 1, ATT_DIM), F32),
                     pl.BlockSpec((nbt, 1, ATT_DIM), lambda i: (i, 0, 0))))
    if gate:
        names.append("lf")
        outs.append((jax.ShapeDtypeStruct((n, LANES), F32), pl.BlockSpec((tm, LANES), lambda i: (i, 0))))
    res = pl.pallas_call(
        functools.partial(_in_proj_kernel, rope=rope, gate=gate, prompt=prompt, kmean=kmean),
        grid=(n // tm,), in_specs=in_specs,
        out_specs=[o[1] for o in outs], out_shape=[o[0] for o in outs],
        compiler_params=_cparams("parallel"), name="in_proj")(*args)
    return dict(zip(names, res))


def _out_proj_kernel(x_ref, o_ref, w_ref, y_ref):
    y_ref[...] = x_ref[...] + jnp.dot(o_ref[...].astype(BF16), w_ref[...], preferred_element_type=F32)


def _out_proj(x2d, o, w_bf):
    n, d = x2d.shape
    tm = _row_tile(n)
    return pl.pallas_call(
        _out_proj_kernel, grid=(n // tm,),
        in_specs=[pl.BlockSpec((tm, d), lambda i: (i, 0)),
                  pl.BlockSpec((tm, ATT_DIM), lambda i: (i, 0)),
                  pl.BlockSpec((ATT_DIM, d), lambda i: (0, 0))],
        out_specs=pl.BlockSpec((tm, d), lambda i: (i, 0)),
        out_shape=jax.ShapeDtypeStruct((n, d), F32),
        compiler_params=_cparams("parallel"), name="out_proj")(x2d, o, w_bf)


def _swiglu_into(act_ref, xb, wgu_ref, d_ff, after_tile=None):
    for t in range(d_ff // FF_TILE):
        lo, hi = t * FF_TILE, (t + 1) * FF_TILE
        g = jnp.dot(xb, wgu_ref[:, lo:hi], preferred_element_type=F32)
        u = jnp.dot(xb, wgu_ref[:, d_ff + lo:d_ff + hi], preferred_element_type=F32)
        act_ref[:, lo:hi] = (g * jax.nn.sigmoid(g) * u).astype(BF16)
        if after_tile is not None:
            after_tile(t)


def _ffn_kernel(x_ref, g_ref, wgu_ref, wd_ref, gfin_ref, y_ref, act_ref, *, d_ff, final_norm):
    x = x_ref[...]
    hn = _rms_norm(x, g_ref[...]).astype(BF16)
    _swiglu_into(act_ref, hn, wgu_ref, d_ff)
    y = x + jnp.dot(act_ref[...], wd_ref[...], preferred_element_type=F32)
    if final_norm:
        y = _rms_norm(y, gfin_ref[...])
    y_ref[...] = y


def _dense_ffn(x2d, g, wgu_bf, wd_bf, g_final, final_norm):
    n, d = x2d.shape
    d_ff = wd_bf.shape[0]
    assert d_ff % FF_TILE == 0
    tm = _row_tile(n)
    return pl.pallas_call(
        functools.partial(_ffn_kernel, d_ff=d_ff, final_norm=final_norm),
        grid=(n // tm,),
        in_specs=[pl.BlockSpec((tm, d), lambda i: (i, 0)),
                  pl.BlockSpec((1, d), lambda i: (0, 0)),
                  pl.BlockSpec((d, 2 * d_ff), lambda i: (0, 0)),
                  pl.BlockSpec((d_ff, d), lambda i: (0, 0)),
                  pl.BlockSpec((1, d), lambda i: (0, 0))],
        out_specs=pl.BlockSpec((tm, d), lambda i: (i, 0)),
        out_shape=jax.ShapeDtypeStruct((n, d), F32),
        scratch_shapes=[pltpu.VMEM((tm, d_ff), BF16)],
        compiler_params=_cparams("parallel"), name="dense_ffn",
    )(x2d, g.reshape(1, d), wgu_bf, wd_bf, g_final.reshape(1, d))


def _router_kernel(x_ref, g_ref, wr_ref, hn_ref, route_ref, *, n_experts):
    hn = _rms_norm(x_ref[...], g_ref[...])
    hn_ref[...] = hn
    logits = jnp.dot(hn, wr_ref[...], preferred_element_type=F32, precision=HIGHEST)
    lane = lax.broadcasted_iota(jnp.int32, logits.shape, 1)
    logits = jnp.where(lane < n_experts, logits, NEG_INF)
    m1 = jnp.max(logits, axis=1, keepdims=True)
    i1 = jnp.min(jnp.where(logits == m1, lane, LANES), axis=1, keepdims=True)
    rest = jnp.where(lane == i1, NEG_INF, logits)
    m2 = jnp.max(rest, axis=1, keepdims=True)
    i2 = jnp.min(jnp.where(rest == m2, lane, LANES), axis=1, keepdims=True)
    e2 = jnp.exp(m2 - m1)
    den = 1.0 + e2
    route_ref[...] = jnp.where(lane == 0, i1.astype(F32),
                     jnp.where(lane == 1, i2.astype(F32),
                     jnp.where(lane == 2, 1.0 / den,
                     jnp.where(lane == 3, e2 / den, 0.0))))


def _router(x2d, g, wr_pad, n_experts):
    n, d = x2d.shape
    tm = _row_tile(n)
    return pl.pallas_call(
        functools.partial(_router_kernel, n_experts=n_experts), grid=(n // tm,),
        in_specs=[pl.BlockSpec((tm, d), lambda i: (i, 0)),
                  pl.BlockSpec((1, d), lambda i: (0, 0)),
                  pl.BlockSpec((d, LANES), lambda i: (0, 0))],
        out_specs=[pl.BlockSpec((tm, d), lambda i: (i, 0)),
                   pl.BlockSpec((tm, LANES), lambda i: (i, 0))],
        out_shape=[jax.ShapeDtypeStruct((n, d), F32), jax.ShapeDtypeStruct((n, LANES), F32)],
        compiler_params=_cparams("parallel"), name="moe_router")(x2d, g.reshape(1, d), wr_pad)


def _row_copy(src_hbm, src_row, dst, dst_row, sem):
    return pltpu.make_async_copy(src_hbm.at[pl.ds(src_row, 1)], dst.at[pl.ds(dst_row, 1)], sem)


def _expert_kernel(be_ref, idx_ref, nxt_ref, hn_hbm, wgu_ref, wd_ref, y_ref, xbuf, act_ref, sems, *, d_ff, rows):
    del be_ref
    i = pl.program_id(0)
    slot = lax.rem(i, 2)

    def fetch(src_idx_ref, r, to_slot):
        return _row_copy(hn_hbm, src_idx_ref[0, 0, r], xbuf.at[to_slot], r, sems.at[to_slot])

    def wait_slot(s):
        for r in range(rows):
            _row_copy(hn_hbm, 0, xbuf.at[s], r, sems.at[s]).wait()

    @pl.when(i == 0)
    def _():
        def start(r, c):
            fetch(idx_ref, r, 0).start()
            return c
        lax.fori_loop(0, rows, start, 0)

    wait_slot(slot)
    xb = xbuf[slot].astype(BF16)
    n_tiles = d_ff // FF_TILE
    per_tile = -(-rows // n_tiles)

    def prefetch(t):
        for r in range(t * per_tile, min(rows, (t + 1) * per_tile)):
            fetch(nxt_ref, r, 1 - slot).start()

    _swiglu_into(act_ref, xb, wgu_ref.at[0], d_ff, after_tile=prefetch)
    y_ref[...] = jnp.dot(act_ref[...], wd_ref[0], preferred_element_type=F32)

    @pl.when(i == pl.num_programs(0) - 1)
    def _():
        wait_slot(1 - slot)


def _expert_ffn(hn, row_tok, blk_e, wgu_bf, wd_bf, rows):
    d = hn.shape[1]
    n_rows = row_tok.shape[0]
    nb = n_rows // rows
    d_ff = wd_bf.shape[1]
    assert d_ff % FF_TILE == 0

    def idx_spec(ahead):
        return pl.BlockSpec((1, 1, rows), lambda i, be: (jnp.minimum(i + ahead, nb - 1), 0, 0),
                            memory_space=pltpu.SMEM)

    grid_spec = pltpu.PrefetchScalarGridSpec(
        num_scalar_prefetch=1, grid=(nb,),
        in_specs=[idx_spec(0), idx_spec(1), pl.BlockSpec(memory_space=pl.ANY),
                  pl.BlockSpec((1, d, 2 * d_ff), lambda i, be: (be[i], 0, 0), pipeline_mode=pl.Buffered(1)),
                  pl.BlockSpec((1, d_ff, d), lambda i, be: (be[i], 0, 0), pipeline_mode=pl.Buffered(1))],
        out_specs=pl.BlockSpec((rows, d), lambda i, be: (i, 0)),
        scratch_shapes=[pltpu.VMEM((2, rows, d), F32), pltpu.VMEM((rows, d_ff), BF16),
                        pltpu.SemaphoreType.DMA((2,))])
    idx3 = row_tok.reshape(nb, 1, rows)
    return pl.pallas_call(
        functools.partial(_expert_kernel, d_ff=d_ff, rows=rows), grid_spec=grid_spec,
        out_shape=jax.ShapeDtypeStruct((n_rows, d), F32),
        compiler_params=_cparams("arbitrary"), name="moe_experts")(blk_e, idx3, idx3, hn, wgu_bf, wd_bf)


def _combine_kernel(rows_ref, nxt_ref, x_ref, route_ref, gfin_ref, ys_hbm, y_ref, buf, sems, *, tm, final_norm):
    i = pl.program_id(0)
    slot = lax.rem(i, 2)

    def fetch(src_rows_ref, r, k, to_slot):
        return _row_copy(ys_hbm, src_rows_ref[0, 0, MOE_TOPK * r + k], buf.at[to_slot, k], r, sems.at[to_slot])

    def start_all(src_rows_ref, to_slot):
        def body(r, c):
            for k in range(MOE_TOPK):
                fetch(src_rows_ref, r, k, to_slot).start()
            return c
        lax.fori_loop(0, tm, body, 0)

    def wait_all(s):
        def body(r, c):
            for k in range(MOE_TOPK):
                _row_copy(ys_hbm, 0, buf.at[s, k], r, sems.at[s]).wait()
            return c
        lax.fori_loop(0, tm, body, 0)

    @pl.when(i == 0)
    def _():
        start_all(rows_ref, 0)

    start_all(nxt_ref, 1 - slot)
    wait_all(slot)
    route = route_ref[...]
    y = x_ref[...] + (buf[slot, 0] * route[:, 2:3] + buf[slot, 1] * route[:, 3:4])
    if final_norm:
        y = _rms_norm(y, gfin_ref[...])
    y_ref[...] = y

    @pl.when(i == pl.num_programs(0) - 1)
    def _():
        wait_all(1 - slot)


def _moe_combine(x2d, route, rowpos, ys, g_final, final_norm):
    n, d = x2d.shape
    tm = min(_row_tile(n), MOE_ROW_BLOCK)
    nb = n // tm

    def rows_spec(ahead):
        return pl.BlockSpec((1, 1, MOE_TOPK * tm), lambda i: (jnp.minimum(i + ahead, nb - 1), 0, 0),
                            memory_space=pltpu.SMEM)

    rows3 = rowpos.reshape(nb, 1, MOE_TOPK * tm)
    return pl.pallas_call(
        functools.partial(_combine_kernel, tm=tm, final_norm=final_norm), grid=(nb,),
        in_specs=[rows_spec(0), rows_spec(1),
                  pl.BlockSpec((tm, d), lambda i: (i, 0)),
                  pl.BlockSpec((tm, LANES), lambda i: (i, 0)),
                  pl.BlockSpec((1, d), lambda i: (0, 0)),
                  pl.BlockSpec(memory_space=pl.ANY)],
        out_specs=pl.BlockSpec((tm, d), lambda i: (i, 0)),
        out_shape=jax.ShapeDtypeStruct((n, d), F32),
        scratch_shapes=[pltpu.VMEM((2, MOE_TOPK, tm, d), F32), pltpu.SemaphoreType.DMA((2,))],
        compiler_params=_cparams("arbitrary"), name="moe_combine",
    )(rows3, rows3, x2d, route, g_final.reshape(1, d), ys)


def _moe_plan(top_idx, n_experts):
    n = top_idx.shape[0]
    n_asg = n * MOE_TOPK
    blk = min(MOE_ROW_BLOCK, n_asg)
    n_rows = -(-(n_asg + n_experts * (blk - 1)) // blk) * blk
    flat_e = top_idx.reshape(-1)
    onehot = (flat_e[:, None] == jnp.arange(n_experts, dtype=jnp.int32)[None, :]).astype(jnp.int32)
    csum = jnp.cumsum(onehot, axis=0)
    counts = csum[-1]
    rank = jnp.take_along_axis(csum, flat_e[:, None], axis=1)[:, 0] - 1
    padded = (counts + blk - 1) // blk * blk
    pad_end = jnp.cumsum(padded)
    pad_start = pad_end - padded
    row = (pad_start[flat_e] + rank).astype(jnp.int32)
    flat_tok = jnp.repeat(jnp.arange(n, dtype=jnp.int32), MOE_TOPK)
    row_tok = jnp.zeros((n_rows,), jnp.int32).at[row].set(flat_tok)
    blk_start = jnp.arange(n_rows // blk, dtype=jnp.int32) * blk
    blk_e = jnp.minimum(jnp.searchsorted(pad_end, blk_start, side="right"), n_experts - 1).astype(jnp.int32)
    return row.reshape(n, MOE_TOPK), row_tok, blk_e, blk


def _moe_ffn(x2d, g, wr_pad, wgu_bf, wd_bf, g_final, final_norm):
    n_experts = wgu_bf.shape[0]
    hn, route = _router(x2d, g, wr_pad, n_experts)
    top_idx = route[:, :MOE_TOPK].astype(jnp.int32)
    rowpos, row_tok, blk_e, blk = _moe_plan(top_idx, n_experts)
    ys = _expert_ffn(hn, row_tok, blk_e, wgu_bf, wd_bf, blk)
    return _moe_combine(x2d, route, rowpos, ys, g_final, final_norm)


def _transpose_v(vt_ref, vb_ref, n_tiles):
    for n in range(n_tiles):
        rows = slice(n * ATT_TILE, (n + 1) * ATT_TILE)
        vt_ref[:, rows] = vb_ref[rows, :].astype(F32).T.astype(BF16)


def _causal_bias():
    key = lax.broadcasted_iota(jnp.int32, (ATT_TILE, ATT_TILE), 0)
    query = lax.broadcasted_iota(jnp.int32, (ATT_TILE, ATT_TILE), 1)
    return jnp.where(key <= query, 0.0, NEG_INF)


def _attend_t(qi, qb_t, kb_ref, vt_ref, causal, past_bias=None, bias=None):
    n_keys = (qi + 1) * ATT_TILE
    s_all = jnp.dot(kb_ref[0:n_keys, :], qb_t, preferred_element_type=F32)
    tiles = []
    for n in range(qi + 1):
        s = s_all[n * ATT_TILE:(n + 1) * ATT_TILE] * ATT_SCALE
        if bias is not None:
            s = s - bias(n)
        if n == qi:
            s = s + causal
        elif past_bias is not None:
            s = s + past_bias(n)
        tiles.append(s)
    m = functools.reduce(jnp.maximum, [jnp.max(s, axis=0, keepdims=True) for s in tiles])
    probs = [jnp.exp(s - m) for s in tiles]
    l = functools.reduce(jnp.add, [jnp.sum(p, axis=0, keepdims=True) for p in probs])
    p_all = jnp.concatenate([p.astype(BF16) for p in probs], axis=0)
    return jnp.dot(vt_ref[:, 0:n_keys], p_all, preferred_element_type=F32) / l


def _moba_attn_kernel(q_ref, kb_ref, vb_ref, km_ref, o_ref, vt_ref, kmp_ref, *, nb):
    _transpose_v(vt_ref, vb_ref, nb)
    nbp = kmp_ref.shape[0]
    kmp_ref[...] = jnp.zeros(kmp_ref.shape, F32)
    kmp_ref[0:nb, :] = km_ref[...].reshape(nb, HEAD_DIM)
    causal = _causal_bias()
    k_top = min(MOBA_TOPK, nb)
    for qi in range(nb):
        rows = slice(qi * ATT_TILE, (qi + 1) * ATT_TILE)
        q_t = q_ref[rows, :].T
        past_bias = None
        if qi > k_top:
            gate = jnp.dot(kmp_ref[...], q_t, preferred_element_type=F32, precision=HIGHEST)
            blk = lax.broadcasted_iota(jnp.int32, gate.shape, 0)
            g = jnp.where(blk < qi, gate, NEG_INF)
            sel = jnp.full(g.shape, NEG_INF, F32)
            for _ in range(k_top):
                m = jnp.max(g, axis=0, keepdims=True)
                idx = jnp.min(jnp.where(g == m, blk, nbp), axis=0, keepdims=True)
                pick = blk == idx
                sel = jnp.where(pick, 0.0, sel)
                g = jnp.where(pick, NEG_INF, g)
            past_bias = lambda n, sel=sel: sel[n:n + 1, :]
        acc_t = _attend_t(qi, q_t.astype(BF16), kb_ref, vt_ref, causal, past_bias)
        o_ref[rows, :] = acc_t.T.astype(o_ref.dtype)


def _moba_prompt_attn(q, kb, vb, km, batch, seq):
    assert seq % MOBA_BLOCK == 0 and ATT_TILE == MOBA_BLOCK
    nb = seq // MOBA_BLOCK
    nbp = -(-nb // SUBLANES) * SUBLANES
    spec = pl.BlockSpec((seq, HEAD_DIM), lambda b, h: (b, h))
    return pl.pallas_call(
        functools.partial(_moba_attn_kernel, nb=nb), grid=(batch, N_HEADS),
        in_specs=[spec, spec, spec, pl.BlockSpec((nb, 1, HEAD_DIM), lambda b, h: (b, 0, h))],
        out_specs=spec,
        out_shape=jax.ShapeDtypeStruct(q.shape, BF16),
        scratch_shapes=[pltpu.VMEM((HEAD_DIM, seq), BF16), pltpu.VMEM((nbp, HEAD_DIM), F32)],
        compiler_params=_cparams("parallel", "parallel"), name="moba_prompt_attn")(q, kb, vb, km)


def _fox_attn_kernel(q_ref, kb_ref, vb_ref, lf_ref, o_ref, vt_ref, call_ref, cb_ref, *, nq):
    head = pl.program_id(1)
    tiles = [slice(n * ATT_TILE, (n + 1) * ATT_TILE) for n in range(nq)]

    @pl.when(head == 0)
    def _():
        tri = (lax.broadcasted_iota(jnp.int32, (ATT_TILE, ATT_TILE), 0)
               >= lax.broadcasted_iota(jnp.int32, (ATT_TILE, ATT_TILE), 1)).astype(F32)
        run = jnp.zeros((1, LANES), F32)
        for rows in tiles:
            c = jnp.dot(tri, lf_ref[rows, :], preferred_element_type=F32, precision=HIGHEST) + run
            call_ref[rows, :] = c
            run = c[ATT_TILE - 1:ATT_TILE, :]

    _transpose_v(vt_ref, vb_ref, nq)
    lane = lax.broadcasted_iota(jnp.int32, (ATT_TILE, LANES), 1)
    for rows in tiles:
        col = jnp.sum(jnp.where(lane == head, call_ref[rows, :], 0.0), axis=1, keepdims=True)
        cb_ref[rows, :] = jnp.broadcast_to(col, (ATT_TILE, LANES))

    def bias(n):
        return jnp.concatenate([cb_ref[tiles[n], :]] * (ATT_TILE // LANES), axis=1)

    causal = _causal_bias()
    for qi in range(nq):
        qb_t = q_ref[tiles[qi], :].astype(F32).T.astype(BF16)
        acc_t = _attend_t(qi, qb_t, kb_ref, vt_ref, causal, bias=bias)
        o_ref[tiles[qi], :] = acc_t.T.astype(o_ref.dtype)


def _fox_prompt_attn(q_bf, kb, vb, lf, batch, seq):
    assert seq % ATT_TILE == 0
    nq = seq // ATT_TILE
    spec = pl.BlockSpec((seq, HEAD_DIM), lambda b, h: (b, h))
    return pl.pallas_call(
        functools.partial(_fox_attn_kernel, nq=nq), grid=(batch, N_HEADS),
        in_specs=[spec, spec, spec, pl.BlockSpec((seq, LANES), lambda b, h: (b, 0))],
        out_specs=spec,
        out_shape=jax.ShapeDtypeStruct(q_bf.shape, BF16),
        scratch_shapes=[pltpu.VMEM((HEAD_DIM, seq), BF16), pltpu.VMEM((seq, LANES), F32),
                        pltpu.VMEM((seq, LANES), F32)],
        compiler_params=_cparams("parallel", "arbitrary"), name="fox_prompt_attn")(q_bf, kb, vb, lf)


def _page_spec(tail, layer, p, per_step):
    zeros = (0,) * len(tail)
    return pl.BlockSpec((1, 1) + tail, lambda b, j, pt: (layer, pt[b, j * per_step + p]) + zeros)


def _tok_spec():
    return pl.BlockSpec((1, N_HEADS, HEAD_DIM), lambda b, *_: (b, 0, 0))


def _moba_dec_stats_kernel(pt_ref, q_ref, *refs, per_step, n_blk):
    del pt_ref
    pages, sel_ref, gate_ref = refs[:per_step], refs[per_step], refs[per_step + 1]
    j = pl.program_id(1)
    lane = lax.broadcasted_iota(jnp.int32, (N_HEADS, LANES), 1)

    @pl.when(j == 0)
    def _():
        gate_ref[...] = jnp.full(gate_ref.shape, NEG_INF, F32)

    q = q_ref[0]
    blk_per_step = per_step // PAGES_PER_BLOCK
    for t in range(blk_per_step):
        ksum = jnp.sum(pages[PAGES_PER_BLOCK * t][0, 0], axis=0)
        for u in range(1, PAGES_PER_BLOCK):
            ksum = ksum + jnp.sum(pages[PAGES_PER_BLOCK * t + u][0, 0], axis=0)
        g = jnp.sum(ksum * q, axis=1, keepdims=True) * (1.0 / MOBA_BLOCK)
        gate_ref[...] = jnp.where(lane == j * blk_per_step + t, g, gate_ref[...])

    @pl.when(j == pl.num_programs(1) - 1)
    def _():
        g = gate_ref[...]
        out = jnp.zeros((N_HEADS, LANES), jnp.int32)
        for r in range(MOBA_TOPK):
            m = jnp.max(g, axis=1, keepdims=True)
            idx = jnp.min(jnp.where(g == m, lane, LANES), axis=1, keepdims=True)
            out = jnp.where(lane == r, idx, out)
            g = jnp.where(lane == idx, NEG_INF, g)
        sel_ref[0] = out


def _moba_dec_select(q3, cache_k, page_table, layer):
    dec_b, n_pages = page_table.shape
    per_step = math.gcd(n_pages, DEC_PAGES_PER_STEP)
    assert per_step % PAGES_PER_BLOCK == 0
    n_blk = n_pages // PAGES_PER_BLOCK
    assert MOBA_TOPK <= n_blk <= LANES
    grid_spec = pltpu.PrefetchScalarGridSpec(
        num_scalar_prefetch=1, grid=(dec_b, n_pages // per_step),
        in_specs=[_tok_spec()]
        + [_page_spec((PAGE_SIZE, N_HEADS, HEAD_DIM), layer, p, per_step) for p in range(per_step)],
        out_specs=pl.BlockSpec((1, N_HEADS, LANES), lambda b, j, pt: (b, 0, 0)),
        scratch_shapes=[pltpu.VMEM((N_HEADS, LANES), F32)])
    return pl.pallas_call(
        functools.partial(_moba_dec_stats_kernel, per_step=per_step, n_blk=n_blk), grid_spec=grid_spec,
        out_shape=jax.ShapeDtypeStruct((dec_b, N_HEADS, LANES), jnp.int32),
        compiler_params=_cparams("parallel", "arbitrary"), name="moba_dec_select",
    )(page_table, q3, *([cache_k] * per_step))


def _moba_dec_attn_kernel(pg_ref, q_ref, kn_ref, vn_ref, ck_hbm, cv_hbm, o_ref, kbuf, vbuf, sems, *, layer, n_sel):
    b = pl.program_id(0)

    def copies(h, t):
        page = pg_ref[(b * N_HEADS + h) * n_sel + t]
        rows = pl.ds(t * PAGE_SIZE, PAGE_SIZE)
        return (pltpu.make_async_copy(ck_hbm.at[layer, page, :, h, :], kbuf.at[h, rows, :], sems.at[0]),
                pltpu.make_async_copy(cv_hbm.at[layer, page, :, h, :], vbuf.at[h, rows, :], sems.at[1]))

    for h in range(N_HEADS):
        for t in range(n_sel):
            for cp in copies(h, t):
                cp.start()
    for h in range(N_HEADS):
        for t in range(n_sel):
            for cp in copies(h, t):
                cp.wait()

    for h in range(N_HEADS):
        q = q_ref[0, h:h + 1, :]
        q8 = jnp.broadcast_to(q, (SUBLANES, HEAD_DIM)).astype(BF16)
        s = _dot_nt(q8, kbuf[h].astype(BF16))[0:1] * ATT_SCALE
        s_new = jnp.sum(q * kn_ref[0, h:h + 1, :], axis=1, keepdims=True) * ATT_SCALE
        m = jnp.maximum(jnp.max(s, axis=1, keepdims=True), s_new)
        p = jnp.exp(s - m)
        p_new = jnp.exp(s_new - m)
        l = jnp.sum(p, axis=1, keepdims=True) + p_new
        p8 = jnp.broadcast_to(p, (SUBLANES, p.shape[1])).astype(BF16)
        pv = jnp.dot(p8, vbuf[h].astype(BF16), preferred_element_type=F32)[0:1]
        o_ref[0, h:h + 1, :] = (pv + p_new * vn_ref[0, h:h + 1, :]) / l


def _moba_dec_attn(q3, k_new3, v_new3, cache_k, cache_v, sel_pages, layer):
    dec_b = q3.shape[0]
    n_sel = MOBA_TOPK * PAGES_PER_BLOCK
    grid_spec = pltpu.PrefetchScalarGridSpec(
        num_scalar_prefetch=1, grid=(dec_b,),
        in_specs=[_tok_spec(), _tok_spec(), _tok_spec(),
                  pl.BlockSpec(memory_space=pl.ANY), pl.BlockSpec(memory_space=pl.ANY)],
        out_specs=_tok_spec(),
        scratch_shapes=[pltpu.VMEM((N_HEADS, n_sel * PAGE_SIZE, HEAD_DIM), F32),
                        pltpu.VMEM((N_HEADS, n_sel * PAGE_SIZE, HEAD_DIM), F32),
                        pltpu.SemaphoreType.DMA((2,))])
    return pl.pallas_call(
        functools.partial(_moba_dec_attn_kernel, layer=layer, n_sel=n_sel), grid_spec=grid_spec,
        out_shape=jax.ShapeDtypeStruct((dec_b, N_HEADS, HEAD_DIM), F32),
        compiler_params=_cparams("arbitrary"), name="moba_dec_attn",
    )(sel_pages, q3, k_new3, v_new3, cache_k, cache_v)


def _fox_dec_kernel(pt_ref, q_ref, kn_ref, vn_ref, lfn_ref, *refs, per_step):
    del pt_ref
    k_pages, v_pages, lf_pages = refs[:per_step], refs[per_step:2 * per_step], refs[2 * per_step:3 * per_step]
    o_ref, m_ref, l_ref, acc_ref, run_ref = refs[3 * per_step:]
    j = pl.program_id(1)
    flat = PAGE_SIZE * N_HEADS

    @pl.when(j == 0)
    def _():
        m_ref[...] = jnp.full(m_ref.shape, NEG_INF, F32)
        l_ref[...] = jnp.zeros(l_ref.shape, F32)
        acc_ref[...] = jnp.zeros(acc_ref.shape, F32)
        run_ref[...] = jnp.zeros(run_ref.shape, F32)

    q = q_ref[0]
    qb = q.astype(BF16)
    lane = lax.broadcasted_iota(jnp.int32, (SUBLANES, LANES), 1)
    sub = lax.broadcasted_iota(jnp.int32, (SUBLANES, LANES), 0)
    col_head = lax.broadcasted_iota(jnp.int32, (N_HEADS, flat), 1) & (N_HEADS - 1)
    own_head = col_head == lax.broadcasted_iota(jnp.int32, (N_HEADS, flat), 0)

    def update(s, pv_of):
        m_old = m_ref[...]
        m_new = jnp.maximum(m_old, jnp.max(s, axis=1, keepdims=True))
        alpha = jnp.exp(m_old - m_new)
        p = jnp.exp(s - m_new)
        l_ref[...] = alpha * l_ref[...] + jnp.sum(p, axis=1, keepdims=True)
        acc_ref[...] = alpha * acc_ref[...] + pv_of(p)
        m_ref[...] = m_new

    run = run_ref[...]
    biases = []
    for t in range(per_step):
        x = lf_pages[t][0, 0]
        tot = x
        d = N_HEADS
        while d < LANES:
            x = x + jnp.where(lane >= d, pltpu.roll(x, d, 1), 0.0)
            tot = tot + pltpu.roll(tot, d, 1)
            d *= 2
        inc = tot
        d = 1
        while d < SUBLANES:
            inc = inc + jnp.where(sub >= d, pltpu.roll(inc, d, 0), 0.0)
            d *= 2
        c = x + (inc - tot) + run
        run = run + inc[SUBLANES - 1:SUBLANES, :]
        biases += [-c[a:a + 1, :] for a in range(SUBLANES)]
    run_ref[...] = run
    bias = jnp.concatenate(biases, axis=1)

    k2 = jnp.concatenate([r[0, 0].reshape(flat, HEAD_DIM).astype(BF16) for r in k_pages], axis=0)
    v2 = jnp.concatenate([r[0, 0].reshape(flat, HEAD_DIM).astype(BF16) for r in v_pages], axis=0)
    own_head = jnp.concatenate([own_head] * per_step, axis=1)
    s = jnp.where(own_head, _dot_nt(qb, k2) * ATT_SCALE + bias, NEG_INF)
    update(s, lambda p: jnp.dot(p.astype(BF16), v2, preferred_element_type=F32))

    @pl.when(j == pl.num_programs(1) - 1)
    def _():
        c_new = run_ref[...] + lfn_ref[0]
        c_col = jnp.sum(jnp.where(lane == sub, c_new, 0.0), axis=1, keepdims=True)
        s_new = jnp.sum(q * kn_ref[0], axis=1, keepdims=True) * ATT_SCALE - c_col
        update(s_new, lambda p: p * vn_ref[0])
        o_ref[0] = acc_ref[...] / l_ref[...]


def _fox_dec_attn(q3, k_new3, v_new3, lf_new, cache_k, cache_v, cache_lf, page_table, layer):
    dec_b, n_pages = page_table.shape
    per_step = math.gcd(n_pages, DEC_PAGES_PER_STEP)
    lf_flat = cache_lf.reshape(cache_lf.shape[0], cache_lf.shape[1], PAGE_SIZE * N_HEADS // LANES, LANES)
    kv_tail = (PAGE_SIZE, N_HEADS, HEAD_DIM)
    grid_spec = pltpu.PrefetchScalarGridSpec(
        num_scalar_prefetch=1, grid=(dec_b, n_pages // per_step),
        in_specs=[_tok_spec(), _tok_spec(), _tok_spec(), pl.BlockSpec((1, 1, LANES), lambda b, j, pt: (b, 0, 0))]
        + [_page_spec(kv_tail, layer, p, per_step) for p in range(per_step)] * 2
        + [_page_spec(lf_flat.shape[2:], layer, p, per_step) for p in range(per_step)],
        out_specs=_tok_spec(),
        scratch_shapes=[pltpu.VMEM((N_HEADS, 1), F32), pltpu.VMEM((N_HEADS, 1), F32),
                        pltpu.VMEM((N_HEADS, HEAD_DIM), F32), pltpu.VMEM((SUBLANES, LANES), F32)])
    return pl.pallas_call(
        functools.partial(_fox_dec_kernel, per_step=per_step), grid_spec=grid_spec,
        out_shape=jax.ShapeDtypeStruct((dec_b, N_HEADS, HEAD_DIM), F32),
        compiler_params=_cparams("parallel", "arbitrary"), name="fox_dec_attn",
    )(page_table, q3, k_new3, v_new3, lf_new.reshape(dec_b, 1, LANES),
      *([cache_k] * per_step), *([cache_v] * per_step), *([lf_flat] * per_step))


def _pad_lanes(w):
    return jnp.pad(w, ((0, 0), (0, LANES - w.shape[1])))


def kernel(x_prompt, x_sample, cache_k_moba, cache_v_moba, cache_k_fox, cache_v_fox, cache_lf_fox, page_table, norm_mixer, norm_ffn, moba_w_qkv, moba_w_o, fox_w_in, fox_b_f, fox_w_o, ffn_w_gu, ffn_w_down, moe_w_router, moe_w_gu, moe_w_down, norm_final):
    batch, seq, d = x_prompt.shape
    dec_b, dec_seq, _ = x_sample.shape
    depth = norm_mixer.shape[0]
    past_len = page_table.shape[1] * PAGE_SIZE
    assert d == ATT_DIM and dec_seq == 1 and past_len % MOBA_BLOCK == 0
    assert cache_k_moba.shape[2:] == (PAGE_SIZE, N_HEADS, HEAD_DIM)

    xp = x_prompt.reshape(batch * seq, d)
    xs = x_sample.reshape(dec_b, d)
    pos_p = jnp.arange(seq, dtype=jnp.int32)
    pos_s = jnp.full((dec_b,), past_len, jnp.int32)
    outs = {name: [] for name in ("km_p", "vm_p", "kf_p", "vf_p", "lf_p", "km_s", "vm_s", "kf_s", "vf_s", "lf_s")}
    heads3 = lambda a: a.reshape(dec_b, N_HEADS, HEAD_DIM)

    for i in range(depth):
        j = i // 2
        last = i == depth - 1
        if i % 2 == 0:
            w_qkv = moba_w_qkv[j].astype(BF16)
            w_o = moba_w_o[j].astype(BF16)
            pp = _in_proj(xp, norm_mixer[i], w_qkv, prompt=True, rope_pos=pos_p, kmean=True)
            ps = _in_proj(xs, norm_mixer[i], w_qkv, prompt=False, rope_pos=pos_s)
            op = _moba_prompt_attn(pp["q"], pp["kb"], pp["vb"], pp["km"], batch, seq)
            q3 = heads3(ps["q"])
            sel = _moba_dec_select(q3, cache_k_moba, page_table, j)
            page_slot = (sel[:, :, :MOBA_TOPK, None] * PAGES_PER_BLOCK
                         + jnp.arange(PAGES_PER_BLOCK, dtype=jnp.int32)).reshape(dec_b, -1)
            sel_pages = jnp.take_along_axis(page_table, page_slot, axis=1).reshape(-1)
            os_ = _moba_dec_attn(q3, heads3(ps["k"]), heads3(ps["v"]), cache_k_moba, cache_v_moba, sel_pages, j)
            xp = _out_proj(xp, op, w_o)
            xs = _out_proj(xs, os_.reshape(dec_b, ATT_DIM), w_o)
            outs["km_p"].append(pp["k"]); outs["vm_p"].append(pp["v"])
            outs["km_s"].append(ps["k"]); outs["vm_s"].append(ps["v"])
            wgu, wd = ffn_w_gu[j].astype(BF16), ffn_w_down[j].astype(BF16)
            xp = _dense_ffn(xp, norm_ffn[i], wgu, wd, norm_final, last)
            xs = _dense_ffn(xs, norm_ffn[i], wgu, wd, norm_final, last)
        else:
            w_in = fox_w_in[j]
            w_qkv = w_in[:, :3 * ATT_DIM].astype(BF16)
            w_f = _pad_lanes(w_in[:, 3 * ATT_DIM:]).astype(BF16)
            b_f = _pad_lanes(fox_b_f[j].reshape(1, N_HEADS))
            w_o = fox_w_o[j].astype(BF16)
            pp = _in_proj(xp, norm_mixer[i], w_qkv, prompt=True, gate_w=w_f, gate_b=b_f, q_dtype=BF16)
            ps = _in_proj(xs, norm_mixer[i], w_qkv, prompt=False, gate_w=w_f, gate_b=b_f)
            op = _fox_prompt_attn(pp["q"], pp["kb"], pp["vb"], pp["lf"], batch, seq)
            os_ = _fox_dec_attn(heads3(ps["q"]), heads3(ps["k"]), heads3(ps["v"]), ps["lf"],
                                cache_k_fox, cache_v_fox, cache_lf_fox, page_table, j)
            xp = _out_proj(xp, op, w_o)
            xs = _out_proj(xs, os_.reshape(dec_b, ATT_DIM), w_o)
            outs["kf_p"].append(pp["k"]); outs["vf_p"].append(pp["v"]); outs["lf_p"].append(pp["lf"][:, :N_HEADS])
            outs["kf_s"].append(ps["k"]); outs["vf_s"].append(ps["v"]); outs["lf_s"].append(ps["lf"][:, :N_HEADS])
            wr = _pad_lanes(moe_w_router[j])
            wgu, wd = moe_w_gu[j].astype(BF16), moe_w_down[j].astype(BF16)
            xp = _moe_ffn(xp, norm_ffn[i], wr, wgu, wd, norm_final, last)
            xs = _moe_ffn(xs, norm_ffn[i], wr, wgu, wd, norm_final, last)

    def rows(name, b, t, tail):
        return jnp.stack(outs[name]).reshape((len(outs[name]), b, t) + tail)

    kv_tail = (N_HEADS, HEAD_DIM)
    return (xp.reshape(batch, seq, d), xs.reshape(dec_b, dec_seq, d),
            rows("km_p", batch, seq, kv_tail), rows("vm_p", batch, seq, kv_tail),
            rows("kf_p", batch, seq, kv_tail), rows("vf_p", batch, seq, kv_tail),
            rows("lf_p", batch, seq, (N_HEADS,)),
            rows("km_s", dec_b, dec_seq, kv_tail), rows("vm_s", dec_b, dec_seq, kv_tail),
            rows("kf_s", dec_b, dec_seq, kv_tail), rows("vf_s", dec_b, dec_seq, kv_tail),
            rows("lf_s", dec_b, dec_seq, (N_HEADS,)))
```

```python
import functools
import math

import jax
import jax.numpy as jnp
from jax import lax
from jax.experimental import pallas as pl
from jax.experimental.pallas import tpu as pltpu

F32 = jnp.float32
BF16 = jnp.bfloat16
HIGHEST = lax.Precision.HIGHEST
NEG_INF = float("-inf")

N_HEADS = 8
HEAD_DIM = 128
ATT_DIM = N_HEADS * HEAD_DIM
ROPE_DIM = HEAD_DIM // 4
ROPE_THETA = 500000.0
MOBA_BLOCK = 256
MOBA_TOPK = 3
PAGE_SIZE = 128
MOE_TOPK = 2
MOE_ROW_BLOCK = 256
RMS_EPS = 1e-6
ATT_SCALE = 1.0 / math.sqrt(HEAD_DIM)

LANES = 128
SUBLANES = 8
V7X_VMEM_LIMIT_BYTES = 56 << 20
ROW_TILE = 512
FF_TILE = 256
ATT_TILE = 256
DEC_PAGES_PER_STEP = 8
PAGES_PER_BLOCK = MOBA_BLOCK // PAGE_SIZE

assert N_HEADS == SUBLANES and HEAD_DIM == LANES


def _cparams(*sem):
    return pltpu.CompilerParams(dimension_semantics=sem, vmem_limit_bytes=V7X_VMEM_LIMIT_BYTES)


def _row_tile(n):
    return ROW_TILE if n % ROW_TILE == 0 else n


def _rms_norm(x, g):
    return x * lax.rsqrt(jnp.mean(x * x, axis=-1, keepdims=True) + RMS_EPS) * g


def _log_sigmoid(z):
    return -(jnp.maximum(-z, 0.0) + jnp.log1p(jnp.exp(-jnp.abs(z))))


def _dot_nt(a, b, **kw):
    return lax.dot_general(a, b, (((1,), (1,)), ((), ())), preferred_element_type=F32, **kw)


def _head(h):
    return slice(h * HEAD_DIM, (h + 1) * HEAD_DIM)


def _in_proj_kernel(*refs, rope, gate, prompt, kmean):
    it = iter(refs)
    x_ref, g_ref, w_ref = next(it), next(it), next(it)
    if rope:
        cos_ref, sa_ref, sb_ref = next(it), next(it), next(it)
    if gate:
        wf_ref, bf_ref = next(it), next(it)
    q_ref, k_ref, v_ref = next(it), next(it), next(it)
    if prompt:
        kb_ref, vb_ref = next(it), next(it)
    if kmean:
        km_ref = next(it)
    tm = x_ref.shape[0]
    hn = _rms_norm(x_ref[...], g_ref[...]).astype(BF16)
    half = ROPE_DIM // 2
    for c in range(3):
        y = jnp.dot(hn, w_ref[:, c * ATT_DIM:(c + 1) * ATT_DIM], preferred_element_type=F32)
        for h in range(N_HEADS):
            yh = y[:, _head(h)]
            if rope and c < 2:
                yh = (yh * cos_ref[...] + pltpu.roll(yh, HEAD_DIM - half, 1) * sa_ref[...]
                      + pltpu.roll(yh, half, 1) * sb_ref[...])
            per_head_rows = pl.ds(h, tm, stride=N_HEADS)
            if c == 0:
                if prompt:
                    q_ref[:, _head(h)] = yh.astype(q_ref.dtype)
                else:
                    q_ref[per_head_rows, :] = yh
            else:
                (k_ref if c == 1 else v_ref)[per_head_rows, :] = yh
                if prompt:
                    (kb_ref if c == 1 else vb_ref)[:, _head(h)] = yh.astype(BF16)
                if kmean and c == 1:
                    for blk in range(tm // MOBA_BLOCK):
                        rows = yh[blk * MOBA_BLOCK:(blk + 1) * MOBA_BLOCK]
                        km_ref[blk, :, _head(h)] = jnp.mean(rows, axis=0, keepdims=True)
    if gate:
        lf_ref = next(it)
        z = jnp.dot(hn, wf_ref[...], preferred_element_type=F32) + bf_ref[...]
        lf_ref[...] = _log_sigmoid(z)


def _rope_tables(pos):
    half = ROPE_DIM // 2
    inv = jnp.power(jnp.float32(ROPE_THETA), -jnp.arange(half, dtype=F32) / half)
    ang = pos.astype(F32)[:, None] * inv[None, :]
    cos, sin = jnp.cos(ang), jnp.sin(ang)
    t = pos.shape[0]
    rest = HEAD_DIM - ROPE_DIM
    cos_t = jnp.concatenate([cos, cos, jnp.ones((t, rest), F32)], axis=1)
    sa_t = jnp.concatenate([-sin, jnp.zeros((t, half + rest), F32)], axis=1)
    sb_t = jnp.concatenate([jnp.zeros((t, half), F32), sin, jnp.zeros((t, rest), F32)], axis=1)
    return cos_t, sa_t, sb_t


def _in_proj(x2d, g, w_bf, *, prompt, rope_pos=None, kmean=False, gate_w=None, gate_b=None, q_dtype=F32):
    n, d = x2d.shape
    tm = _row_tile(n)
    rope, gate = rope_pos is not None, gate_w is not None
    assert not kmean or (prompt and tm % MOBA_BLOCK == 0)
    args = [x2d, g.reshape(1, d), w_bf]
    in_specs = [pl.BlockSpec((tm, d), lambda i: (i, 0)), _resident((1, d)), _resident((d, 3 * ATT_DIM))]
    if rope:
        nt = rope_pos.shape[0] // tm
        args += list(_rope_tables(rope_pos))
        in_specs += [pl.BlockSpec((tm, HEAD_DIM), lambda i: (i % nt, 0))] * 3
    if gate:
        args += [gate_w, gate_b]
        in_specs += [_resident((d, LANES)), _resident((1, LANES))]

    rows2d = (jax.ShapeDtypeStruct((n, ATT_DIM), F32), pl.BlockSpec((tm, ATT_DIM), lambda i: (i, 0)))
    head_rows = (jax.ShapeDtypeStruct((n * N_HEADS, HEAD_DIM), F32),
                 pl.BlockSpec((tm * N_HEADS, HEAD_DIM), lambda i: (i, 0)))
    names = ["q", "k", "v"]
    outs = [(jax.ShapeDtypeStruct((n, ATT_DIM), q_dtype), rows2d[1]) if prompt else head_rows, head_rows, head_rows]
    if prompt:
        names += ["kb", "vb"]
        outs += [(jax.ShapeDtypeStruct((n, ATT_DIM), BF16), rows2d[1])] * 2
    if kmean:
        names.append("km")
        nbt = tm // MOBA_BLOCK
        outs.append((jax.ShapeDtypeStruct((n // MOBA_BLOCK, 1, ATT_DIM), F32),
                     pl.BlockSpec((nbt, 1, ATT_DIM), lambda i: (i, 0, 0))))
    if gate:
        names.append("lf")
        outs.append((jax.ShapeDtypeStruct((n, LANES), F32), pl.BlockSpec((tm, LANES), lambda i: (i, 0))))
    res = pl.pallas_call(
        functools.partial(_in_proj_kernel, rope=rope, gate=gate, prompt=prompt, kmean=kmean),
        grid=(n // tm,), in_specs=in_specs,
        out_specs=[o[1] for o in outs], out_shape=[o[0] for o in outs],
        compiler_params=_cparams("parallel"), name="in_proj")(*args)
    return dict(zip(names, res))


def _resident(shape):
    return pl.BlockSpec(shape, lambda *_: (0,) * len(shape), pipeline_mode=pl.Buffered(1))


def _mixer_residual(x_ref, o_ref, wo_ref):
    return x_ref[...] + jnp.dot(o_ref[...].astype(BF16), wo_ref[...], preferred_element_type=F32)

def _swiglu_into(act_ref, xb, wgu_ref, d_ff, after_tile=None):
    for t in range(d_ff // FF_TILE):
        lo, hi = t * FF_TILE, (t + 1) * FF_TILE
        g = jnp.dot(xb, wgu_ref[:, lo:hi], preferred_element_type=F32)
        u = jnp.dot(xb, wgu_ref[:, d_ff + lo:d_ff + hi], preferred_element_type=F32)
        act_ref[:, lo:hi] = (g * jax.nn.sigmoid(g) * u).astype(BF16)
        if after_tile is not None:
            after_tile(t)


def _ffn_kernel(x_ref, o_ref, wo_ref, g_ref, wgu_ref, wd_ref, gfin_ref, y_ref, act_ref, *, d_ff, final_norm):
    x = _mixer_residual(x_ref, o_ref, wo_ref)
    hn = _rms_norm(x, g_ref[...]).astype(BF16)
    _swiglu_into(act_ref, hn, wgu_ref, d_ff)
    y = x + jnp.dot(act_ref[...], wd_ref[...], preferred_element_type=F32)
    if final_norm:
        y = _rms_norm(y, gfin_ref[...])
    y_ref[...] = y


def _dense_ffn(x2d, o, wo_bf, g, wgu_bf, wd_bf, g_final, final_norm):
    n, d = x2d.shape
    d_ff = wd_bf.shape[0]
    assert d_ff % FF_TILE == 0
    tm = _row_tile(n)
    rows = lambda width: pl.BlockSpec((tm, width), lambda i: (i, 0))
    return pl.pallas_call(
        functools.partial(_ffn_kernel, d_ff=d_ff, final_norm=final_norm),
        grid=(n // tm,),
        in_specs=[rows(d), rows(ATT_DIM), _resident((ATT_DIM, d)), _resident((1, d)),
                  _resident((d, 2 * d_ff)), _resident((d_ff, d)), _resident((1, d))],
        out_specs=rows(d),
        out_shape=jax.ShapeDtypeStruct((n, d), F32),
        scratch_shapes=[pltpu.VMEM((tm, d_ff), BF16)],
        compiler_params=_cparams("parallel"), name="dense_ffn",
    )(x2d, o, wo_bf, g.reshape(1, d), wgu_bf, wd_bf, g_final.reshape(1, d))


def _router_kernel(x_ref, o_ref, wo_ref, g_ref, wr_ref, xm_ref, hn_ref, route_ref, *, n_experts):
    x = _mixer_residual(x_ref, o_ref, wo_ref)
    xm_ref[...] = x
    hn = _rms_norm(x, g_ref[...])
    hn_ref[...] = hn
    logits = jnp.dot(hn, wr_ref[...], preferred_element_type=F32, precision=HIGHEST)
    lane = lax.broadcasted_iota(jnp.int32, logits.shape, 1)
    logits = jnp.where(lane < n_experts, logits, NEG_INF)
    m1 = jnp.max(logits, axis=1, keepdims=True)
    i1 = jnp.min(jnp.where(logits == m1, lane, LANES), axis=1, keepdims=True)
    rest = jnp.where(lane == i1, NEG_INF, logits)
    m2 = jnp.max(rest, axis=1, keepdims=True)
    i2 = jnp.min(jnp.where(rest == m2, lane, LANES), axis=1, keepdims=True)
    e2 = jnp.exp(m2 - m1)
    den = 1.0 + e2
    route_ref[...] = jnp.where(lane == 0, i1.astype(F32),
                     jnp.where(lane == 1, i2.astype(F32),
                     jnp.where(lane == 2, 1.0 / den,
                     jnp.where(lane == 3, e2 / den, 0.0))))


def _router(x2d, o, wo_bf, g, wr_pad, n_experts):
    n, d = x2d.shape
    tm = _row_tile(n)
    rows = lambda width: pl.BlockSpec((tm, width), lambda i: (i, 0))
    return pl.pallas_call(
        functools.partial(_router_kernel, n_experts=n_experts), grid=(n // tm,),
        in_specs=[rows(d), rows(ATT_DIM), _resident((ATT_DIM, d)), _resident((1, d)), _resident((d, LANES))],
        out_specs=[rows(d), rows(d), rows(LANES)],
        out_shape=[jax.ShapeDtypeStruct((n, d), F32), jax.ShapeDtypeStruct((n, d), F32),
                   jax.ShapeDtypeStruct((n, LANES), F32)],
        compiler_params=_cparams("parallel"), name="moe_router")(x2d, o, wo_bf, g.reshape(1, d), wr_pad)


def _row_copy(src_hbm, src_row, dst, dst_row, sem):
    return pltpu.make_async_copy(src_hbm.at[pl.ds(src_row, 1)], dst.at[pl.ds(dst_row, 1)], sem)


def _expert_kernel(be_ref, src_ref, nxt_ref, dprev_ref, dcur_ref, hn_hbm, wgu_ref, wd_ref, y_hbm,
                   xbuf, ybuf, act_ref, gsem, ssem, *, d_ff, rows):
    del be_ref
    i = pl.program_id(0)
    slot = lax.rem(i, 2)
    other = 1 - slot

    def gather(idx_ref, r, s):
        return _row_copy(hn_hbm, idx_ref[0, 0, r], xbuf.at[s], r, gsem.at[s])

    def scatter(dst_ref, r, s):
        return pltpu.make_async_copy(ybuf.at[s, pl.ds(r, 1)], y_hbm.at[pl.ds(dst_ref[0, 0, r], 1)], ssem.at[s])

    def wait_gather(s):
        for r in range(rows):
            _row_copy(hn_hbm, 0, xbuf.at[s], r, gsem.at[s]).wait()

    def wait_scatter(s):
        for r in range(rows):
            pltpu.make_async_copy(ybuf.at[s, pl.ds(r, 1)], y_hbm.at[pl.ds(0, 1)], ssem.at[s]).wait()

    def start_all(make):
        def body(r, c):
            make(r).start()
            return c
        lax.fori_loop(0, rows, body, 0)

    @pl.when(i == 0)
    def _():
        start_all(lambda r: gather(src_ref, r, 0))
        ybuf[1] = jnp.zeros(ybuf.shape[1:], F32)

    @pl.when(i > 0)
    def _():
        wait_scatter(slot)

    wait_gather(slot)
    xb = xbuf[slot].astype(BF16)
    n_tiles = d_ff // FF_TILE
    per_tile = -(-rows // n_tiles)

    def side_dmas(t):
        for r in range(t * per_tile, min(rows, (t + 1) * per_tile)):
            gather(nxt_ref, r, other).start()
            scatter(dprev_ref, r, other).start()

    _swiglu_into(act_ref, xb, wgu_ref.at[0], d_ff, after_tile=side_dmas)
    ybuf[slot] = jnp.dot(act_ref[...], wd_ref[0], preferred_element_type=F32)

    @pl.when(i == pl.num_programs(0) - 1)
    def _():
        wait_gather(other)
        wait_scatter(other)
        start_all(lambda r: scatter(dcur_ref, r, slot))
        wait_scatter(slot)


def _expert_ffn(hn, row_src, row_dst, blk_e, wgu_bf, wd_bf, rows):
    d = hn.shape[1]
    n_rows = row_src.shape[0]
    nb = n_rows // rows
    d_ff = wd_bf.shape[1]
    assert d_ff % FF_TILE == 0

    def idx_spec(ahead):
        return pl.BlockSpec((1, 1, rows), lambda i, be: (jnp.minimum(i + ahead, nb - 1), 0, 0),
                            memory_space=pltpu.SMEM)

    def dst_spec(ahead):
        return pl.BlockSpec((1, 1, rows), lambda i, be: (i + ahead, 0, 0), memory_space=pltpu.SMEM)

    grid_spec = pltpu.PrefetchScalarGridSpec(
        num_scalar_prefetch=1, grid=(nb,),
        in_specs=[idx_spec(0), idx_spec(1), dst_spec(0), dst_spec(1), pl.BlockSpec(memory_space=pl.ANY),
                  pl.BlockSpec((1, d, 2 * d_ff), lambda i, be: (be[i], 0, 0), pipeline_mode=pl.Buffered(1)),
                  pl.BlockSpec((1, d_ff, d), lambda i, be: (be[i], 0, 0), pipeline_mode=pl.Buffered(1))],
        out_specs=pl.BlockSpec(memory_space=pl.ANY),
        scratch_shapes=[pltpu.VMEM((2, rows, d), F32), pltpu.VMEM((2, rows, d), F32),
                        pltpu.VMEM((rows, d_ff), BF16),
                        pltpu.SemaphoreType.DMA((2,)), pltpu.SemaphoreType.DMA((2,))])
    src3 = row_src.reshape(nb, 1, rows)
    spare = n_rows + jnp.arange(rows, dtype=jnp.int32)
    dst3 = jnp.concatenate([spare, row_dst]).reshape(nb + 1, 1, rows)
    return pl.pallas_call(
        functools.partial(_expert_kernel, d_ff=d_ff, rows=rows), grid_spec=grid_spec,
        out_shape=jax.ShapeDtypeStruct((n_rows + rows, d), F32),
        compiler_params=_cparams("arbitrary"), name="moe_experts",
    )(blk_e, src3, src3, dst3, dst3, hn, wgu_bf, wd_bf)


def _combine_kernel(x_ref, y0_ref, y1_ref, route_ref, gfin_ref, o_ref, *, final_norm):
    route = route_ref[...]
    y = x_ref[...] + (y0_ref[...] * route[:, 2:3] + y1_ref[...] * route[:, 3:4])
    if final_norm:
        y = _rms_norm(y, gfin_ref[...])
    o_ref[...] = y


def _moe_combine(x2d, route, y, g_final, final_norm):
    n, d = x2d.shape
    tm = _row_tile(n)
    nb = n // tm
    return pl.pallas_call(
        functools.partial(_combine_kernel, final_norm=final_norm), grid=(nb,),
        in_specs=[pl.BlockSpec((tm, d), lambda i: (i, 0)),
                  pl.BlockSpec((tm, d), lambda i: (i, 0)),
                  pl.BlockSpec((tm, d), lambda i: (nb + i, 0)),
                  pl.BlockSpec((tm, LANES), lambda i: (i, 0)),
                  _resident((1, d))],
        out_specs=pl.BlockSpec((tm, d), lambda i: (i, 0)),
        out_shape=jax.ShapeDtypeStruct((n, d), F32),
        compiler_params=_cparams("parallel"), name="moe_combine",
    )(x2d, y, y, route, g_final.reshape(1, d))


def _moe_plan(top_idx, n_experts):
    n = top_idx.shape[0]
    n_asg = n * MOE_TOPK
    blk = min(MOE_ROW_BLOCK, n_asg)
    n_rows = -(-(n_asg + n_experts * (blk - 1)) // blk) * blk
    flat_e = top_idx.reshape(-1)
    onehot = (flat_e[:, None] == jnp.arange(n_experts, dtype=jnp.int32)[None, :]).astype(jnp.int32)
    csum = jnp.cumsum(onehot, axis=0)
    counts = csum[-1]
    rank = jnp.take_along_axis(csum, flat_e[:, None], axis=1)[:, 0] - 1
    padded = (counts + blk - 1) // blk * blk
    pad_end = jnp.cumsum(padded)
    pad_start = pad_end - padded
    row = (pad_start[flat_e] + rank).astype(jnp.int32)
    row_asg = jnp.full((n_rows,), -1, jnp.int32).at[row].set(jnp.arange(n_asg, dtype=jnp.int32))
    is_pad = row_asg < 0
    tok = jnp.where(is_pad, 0, row_asg // MOE_TOPK)
    pad_rank = jnp.cumsum(is_pad.astype(jnp.int32)) - 1
    row_dst = jnp.where(is_pad, n_asg + pad_rank, (row_asg % MOE_TOPK) * n + tok).astype(jnp.int32)
    blk_start = jnp.arange(n_rows // blk, dtype=jnp.int32) * blk
    blk_e = jnp.minimum(jnp.searchsorted(pad_end, blk_start, side="right"), n_experts - 1).astype(jnp.int32)
    return tok.astype(jnp.int32), row_dst, blk_e, blk


def _moe_ffn(x2d, o, wo_bf, g, wr_pad, wgu_bf, wd_bf, g_final, final_norm):
    n_experts = wgu_bf.shape[0]
    xm, hn, route = _router(x2d, o, wo_bf, g, wr_pad, n_experts)
    top_idx = route[:, :MOE_TOPK].astype(jnp.int32)
    row_src, row_dst, blk_e, blk = _moe_plan(top_idx, n_experts)
    y = _expert_ffn(hn, row_src, row_dst, blk_e, wgu_bf, wd_bf, blk)
    return _moe_combine(xm, route, y, g_final, final_norm)


def _transpose_v(vt_ref, vb_ref, n_tiles):
    for n in range(n_tiles):
        rows = slice(n * ATT_TILE, (n + 1) * ATT_TILE)
        vt_ref[:, rows] = vb_ref[rows, :].astype(F32).T.astype(BF16)


def _causal_bias():
    key = lax.broadcasted_iota(jnp.int32, (ATT_TILE, ATT_TILE), 0)
    query = lax.broadcasted_iota(jnp.int32, (ATT_TILE, ATT_TILE), 1)
    return jnp.where(key <= query, 0.0, NEG_INF)


def _attend_t(qi, qb_t, kb_ref, vt_ref, causal, past_bias=None, bias=None):
    n_keys = (qi + 1) * ATT_TILE
    s_all = jnp.dot(kb_ref[0:n_keys, :], qb_t, preferred_element_type=F32)
    tiles = []
    for n in range(qi + 1):
        s = s_all[n * ATT_TILE:(n + 1) * ATT_TILE] * ATT_SCALE
        if bias is not None:
            s = s - bias(n)
        if n == qi:
            s = s + causal
        elif past_bias is not None:
            s = s + past_bias(n)
        tiles.append(s)
    m = functools.reduce(jnp.maximum, [jnp.max(s, axis=0, keepdims=True) for s in tiles])
    probs = [jnp.exp(s - m) for s in tiles]
    l = functools.reduce(jnp.add, [jnp.sum(p, axis=0, keepdims=True) for p in probs])
    p_all = jnp.concatenate([p.astype(BF16) for p in probs], axis=0)
    return jnp.dot(vt_ref[:, 0:n_keys], p_all, preferred_element_type=F32) / l


def _moba_attn_kernel(q_ref, kb_ref, vb_ref, km_ref, o_ref, vt_ref, kmp_ref, *, nb):
    _transpose_v(vt_ref, vb_ref, nb)
    nbp = kmp_ref.shape[0]
    kmp_ref[...] = jnp.zeros(kmp_ref.shape, F32)
    kmp_ref[0:nb, :] = km_ref[...].reshape(nb, HEAD_DIM)
    causal = _causal_bias()
    k_top = min(MOBA_TOPK, nb)
    for qi in range(nb):
        rows = slice(qi * ATT_TILE, (qi + 1) * ATT_TILE)
        q_t = q_ref[rows, :].T
        past_bias = None
        if qi > k_top:
            gate = jnp.dot(kmp_ref[...], q_t, preferred_element_type=F32, precision=HIGHEST)
            blk = lax.broadcasted_iota(jnp.int32, gate.shape, 0)
            g = jnp.where(blk < qi, gate, NEG_INF)
            sel = jnp.full(g.shape, NEG_INF, F32)
            for _ in range(k_top):
                m = jnp.max(g, axis=0, keepdims=True)
                idx = jnp.min(jnp.where(g == m, blk, nbp), axis=0, keepdims=True)
                pick = blk == idx
                sel = jnp.where(pick, 0.0, sel)
                g = jnp.where(pick, NEG_INF, g)
            past_bias = lambda n, sel=sel: sel[n:n + 1, :]
        acc_t = _attend_t(qi, q_t.astype(BF16), kb_ref, vt_ref, causal, past_bias)
        o_ref[rows, :] = acc_t.T.astype(o_ref.dtype)


def _moba_prompt_attn(q, kb, vb, km, batch, seq):
    assert seq % MOBA_BLOCK == 0 and ATT_TILE == MOBA_BLOCK
    nb = seq // MOBA_BLOCK
    nbp = -(-nb // SUBLANES) * SUBLANES
    spec = pl.BlockSpec((seq, HEAD_DIM), lambda b, h: (b, h))
    return pl.pallas_call(
        functools.partial(_moba_attn_kernel, nb=nb), grid=(batch, N_HEADS),
        in_specs=[spec, spec, spec, pl.BlockSpec((nb, 1, HEAD_DIM), lambda b, h: (b, 0, h))],
        out_specs=spec,
        out_shape=jax.ShapeDtypeStruct(q.shape, BF16),
        scratch_shapes=[pltpu.VMEM((HEAD_DIM, seq), BF16), pltpu.VMEM((nbp, HEAD_DIM), F32)],
        compiler_params=_cparams("parallel", "parallel"), name="moba_prompt_attn")(q, kb, vb, km)


def _fox_attn_kernel(q_ref, kb_ref, vb_ref, lf_ref, o_ref, vt_ref, call_ref, cb_ref, *, nq):
    head = pl.program_id(1)
    tiles = [slice(n * ATT_TILE, (n + 1) * ATT_TILE) for n in range(nq)]

    @pl.when(head == 0)
    def _():
        tri = (lax.broadcasted_iota(jnp.int32, (ATT_TILE, ATT_TILE), 0)
               >= lax.broadcasted_iota(jnp.int32, (ATT_TILE, ATT_TILE), 1)).astype(F32)
        run = jnp.zeros((1, LANES), F32)
        for rows in tiles:
            c = jnp.dot(tri, lf_ref[rows, :], preferred_element_type=F32, precision=HIGHEST) + run
            call_ref[rows, :] = c
            run = c[ATT_TILE - 1:ATT_TILE, :]

    _transpose_v(vt_ref, vb_ref, nq)
    lane = lax.broadcasted_iota(jnp.int32, (ATT_TILE, LANES), 1)
    for rows in tiles:
        col = jnp.sum(jnp.where(lane == head, call_ref[rows, :], 0.0), axis=1, keepdims=True)
        cb_ref[rows, :] = jnp.broadcast_to(col, (ATT_TILE, LANES))

    def bias(n):
        return jnp.concatenate([cb_ref[tiles[n], :]] * (ATT_TILE // LANES), axis=1)

    causal = _causal_bias()
    for qi in range(nq):
        qb_t = q_ref[tiles[qi], :].astype(F32).T.astype(BF16)
        acc_t = _attend_t(qi, qb_t, kb_ref, vt_ref, causal, bias=bias)
        o_ref[tiles[qi], :] = acc_t.T.astype(o_ref.dtype)


def _fox_prompt_attn(q_bf, kb, vb, lf, batch, seq):
    assert seq % ATT_TILE == 0
    nq = seq // ATT_TILE
    spec = pl.BlockSpec((seq, HEAD_DIM), lambda b, h: (b, h))
    return pl.pallas_call(
        functools.partial(_fox_attn_kernel, nq=nq), grid=(batch, N_HEADS),
        in_specs=[spec, spec, spec, pl.BlockSpec((seq, LANES), lambda b, h: (b, 0))],
        out_specs=spec,
        out_shape=jax.ShapeDtypeStruct(q_bf.shape, BF16),
        scratch_shapes=[pltpu.VMEM((HEAD_DIM, seq), BF16), pltpu.VMEM((seq, LANES), F32),
                        pltpu.VMEM((seq, LANES), F32)],
        compiler_params=_cparams("parallel", "arbitrary"), name="fox_prompt_attn")(q_bf, kb, vb, lf)


def _page_spec(tail, layer, p, per_step):
    zeros = (0,) * len(tail)
    return pl.BlockSpec((1, 1) + tail, lambda b, j, pt: (layer, pt[b, j * per_step + p]) + zeros)


def _tok_spec():
    return pl.BlockSpec((1, N_HEADS, HEAD_DIM), lambda b, *_: (b, 0, 0))


def _moba_dec_stats_kernel(pt_ref, q_ref, *refs, per_step, n_blk):
    del pt_ref
    pages, sel_ref, gate_ref = refs[:per_step], refs[per_step], refs[per_step + 1]
    j = pl.program_id(1)
    lane = lax.broadcasted_iota(jnp.int32, (N_HEADS, LANES), 1)

    @pl.when(j == 0)
    def _():
        gate_ref[...] = jnp.full(gate_ref.shape, NEG_INF, F32)

    q = q_ref[0]
    blk_per_step = per_step // PAGES_PER_BLOCK
    for t in range(blk_per_step):
        ksum = jnp.sum(pages[PAGES_PER_BLOCK * t][0, 0], axis=0)
        for u in range(1, PAGES_PER_BLOCK):
            ksum = ksum + jnp.sum(pages[PAGES_PER_BLOCK * t + u][0, 0], axis=0)
        g = jnp.sum(ksum * q, axis=1, keepdims=True) * (1.0 / MOBA_BLOCK)
        gate_ref[...] = jnp.where(lane == j * blk_per_step + t, g, gate_ref[...])

    @pl.when(j == pl.num_programs(1) - 1)
    def _():
        g = gate_ref[...]
        out = jnp.zeros((N_HEADS, LANES), jnp.int32)
        for r in range(MOBA_TOPK):
            m = jnp.max(g, axis=1, keepdims=True)
            idx = jnp.min(jnp.where(g == m, lane, LANES), axis=1, keepdims=True)
            out = jnp.where(lane == r, idx, out)
            g = jnp.where(lane == idx, NEG_INF, g)
        sel_ref[0] = out


def _moba_dec_select(q3, cache_k, page_table, layer):
    dec_b, n_pages = page_table.shape
    per_step = math.gcd(n_pages, DEC_PAGES_PER_STEP)
    assert per_step % PAGES_PER_BLOCK == 0
    n_blk = n_pages // PAGES_PER_BLOCK
    assert MOBA_TOPK <= n_blk <= LANES
    grid_spec = pltpu.PrefetchScalarGridSpec(
        num_scalar_prefetch=1, grid=(dec_b, n_pages // per_step),
        in_specs=[_tok_spec()]
        + [_page_spec((PAGE_SIZE, N_HEADS, HEAD_DIM), layer, p, per_step) for p in range(per_step)],
        out_specs=pl.BlockSpec((1, N_HEADS, LANES), lambda b, j, pt: (b, 0, 0)),
        scratch_shapes=[pltpu.VMEM((N_HEADS, LANES), F32)])
    return pl.pallas_call(
        functools.partial(_moba_dec_stats_kernel, per_step=per_step, n_blk=n_blk), grid_spec=grid_spec,
        out_shape=jax.ShapeDtypeStruct((dec_b, N_HEADS, LANES), jnp.int32),
        compiler_params=_cparams("parallel", "arbitrary"), name="moba_dec_select",
    )(page_table, q3, *([cache_k] * per_step))


def _moba_dec_attn_kernel(pg_ref, q_ref, kn_ref, vn_ref, ck_hbm, cv_hbm, o_ref, kbuf, vbuf, sems, *, layer, n_sel):
    b = pl.program_id(0)

    def copies(h, t):
        page = pg_ref[(b * N_HEADS + h) * n_sel + t]
        rows = pl.ds(t * PAGE_SIZE, PAGE_SIZE)
        return (pltpu.make_async_copy(ck_hbm.at[layer, page, :, h, :], kbuf.at[h, rows, :], sems.at[0]),
                pltpu.make_async_copy(cv_hbm.at[layer, page, :, h, :], vbuf.at[h, rows, :], sems.at[1]))

    for h in range(N_HEADS):
        for t in range(n_sel):
            for cp in copies(h, t):
                cp.start()
    for h in range(N_HEADS):
        for t in range(n_sel):
            for cp in copies(h, t):
                cp.wait()

    for h in range(N_HEADS):
        q = q_ref[0, h:h + 1, :]
        q8 = jnp.broadcast_to(q, (SUBLANES, HEAD_DIM)).astype(BF16)
        s = _dot_nt(q8, kbuf[h].astype(BF16))[0:1] * ATT_SCALE
        s_new = jnp.sum(q * kn_ref[0, h:h + 1, :], axis=1, keepdims=True) * ATT_SCALE
        m = jnp.maximum(jnp.max(s, axis=1, keepdims=True), s_new)
        p = jnp.exp(s - m)
        p_new = jnp.exp(s_new - m)
        l = jnp.sum(p, axis=1, keepdims=True) + p_new
        p8 = jnp.broadcast_to(p, (SUBLANES, p.shape[1])).astype(BF16)
        pv = jnp.dot(p8, vbuf[h].astype(BF16), preferred_element_type=F32)[0:1]
        o_ref[0, h:h + 1, :] = (pv + p_new * vn_ref[0, h:h + 1, :]) / l


def _moba_dec_attn(q3, k_new3, v_new3, cache_k, cache_v, sel_pages, layer):
    dec_b = q3.shape[0]
    n_sel = MOBA_TOPK * PAGES_PER_BLOCK
    grid_spec = pltpu.PrefetchScalarGridSpec(
        num_scalar_prefetch=1, grid=(dec_b,),
        in_specs=[_tok_spec(), _tok_spec(), _tok_spec(),
                  pl.BlockSpec(memory_space=pl.ANY), pl.BlockSpec(memory_space=pl.ANY)],
        out_specs=_tok_spec(),
        scratch_shapes=[pltpu.VMEM((N_HEADS, n_sel * PAGE_SIZE, HEAD_DIM), F32),
                        pltpu.VMEM((N_HEADS, n_sel * PAGE_SIZE, HEAD_DIM), F32),
                        pltpu.SemaphoreType.DMA((2,))])
    return pl.pallas_call(
        functools.partial(_moba_dec_attn_kernel, layer=layer, n_sel=n_sel), grid_spec=grid_spec,
        out_shape=jax.ShapeDtypeStruct((dec_b, N_HEADS, HEAD_DIM), F32),
        compiler_params=_cparams("arbitrary"), name="moba_dec_attn",
    )(sel_pages, q3, k_new3, v_new3, cache_k, cache_v)


def _fox_dec_kernel(pt_ref, q_ref, kn_ref, vn_ref, lfn_ref, *refs, per_step):
    del pt_ref
    k_pages, v_pages, lf_pages = refs[:per_step], refs[per_step:2 * per_step], refs[2 * per_step:3 * per_step]
    o_ref, m_ref, l_ref, acc_ref, run_ref = refs[3 * per_step:]
    j = pl.program_id(1)
    flat = PAGE_SIZE * N_HEADS

    @pl.when(j == 0)
    def _():
        m_ref[...] = jnp.full(m_ref.shape, NEG_INF, F32)
        l_ref[...] = jnp.zeros(l_ref.shape, F32)
        acc_ref[...] = jnp.zeros(acc_ref.shape, F32)
        run_ref[...] = jnp.zeros(run_ref.shape, F32)

    q = q_ref[0]
    qb = q.astype(BF16)
    lane = lax.broadcasted_iota(jnp.int32, (SUBLANES, LANES), 1)
    sub = lax.broadcasted_iota(jnp.int32, (SUBLANES, LANES), 0)
    col_head = lax.broadcasted_iota(jnp.int32, (N_HEADS, flat), 1) & (N_HEADS - 1)
    own_head = col_head == lax.broadcasted_iota(jnp.int32, (N_HEADS, flat), 0)

    def update(s, pv_of):
        m_old = m_ref[...]
        m_new = jnp.maximum(m_old, jnp.max(s, axis=1, keepdims=True))
        alpha = jnp.exp(m_old - m_new)
        p = jnp.exp(s - m_new)
        l_ref[...] = alpha * l_ref[...] + jnp.sum(p, axis=1, keepdims=True)
        acc_ref[...] = alpha * acc_ref[...] + pv_of(p)
        m_ref[...] = m_new

    run = run_ref[...]
    biases = []
    for t in range(per_step):
        x = lf_pages[t][0, 0]
        tot = x
        d = N_HEADS
        while d < LANES:
            x = x + jnp.where(lane >= d, pltpu.roll(x, d, 1), 0.0)
            tot = tot + pltpu.roll(tot, d, 1)
            d *= 2
        inc = tot
        d = 1
        while d < SUBLANES:
            inc = inc + jnp.where(sub >= d, pltpu.roll(inc, d, 0), 0.0)
            d *= 2
        c = x + (inc - tot) + run
        run = run + inc[SUBLANES - 1:SUBLANES, :]
        biases += [-c[a:a + 1, :] for a in range(SUBLANES)]
    run_ref[...] = run
    bias = jnp.concatenate(biases, axis=1)

    k2 = jnp.concatenate([r[0, 0].reshape(flat, HEAD_DIM).astype(BF16) for r in k_pages], axis=0)
    v2 = jnp.concatenate([r[0, 0].reshape(flat, HEAD_DIM).astype(BF16) for r in v_pages], axis=0)
    own_head = jnp.concatenate([own_head] * per_step, axis=1)
    s = jnp.where(own_head, _dot_nt(qb, k2) * ATT_SCALE + bias, NEG_INF)
    update(s, lambda p: jnp.dot(p.astype(BF16), v2, preferred_element_type=F32))

    @pl.when(j == pl.num_programs(1) - 1)
    def _():
        c_new = run_ref[...] + lfn_ref[0]
        c_col = jnp.sum(jnp.where(lane == sub, c_new, 0.0), axis=1, keepdims=True)
        s_new = jnp.sum(q * kn_ref[0], axis=1, keepdims=True) * ATT_SCALE - c_col
        update(s_new, lambda p: p * vn_ref[0])
        o_ref[0] = acc_ref[...] / l_ref[...]


def _fox_dec_attn(q3, k_new3, v_new3, lf_new, cache_k, cache_v, cache_lf, page_table, layer):
    dec_b, n_pages = page_table.shape
    per_step = math.gcd(n_pages, DEC_PAGES_PER_STEP)
    lf_flat = cache_lf.reshape(cache_lf.shape[0], cache_lf.shape[1], PAGE_SIZE * N_HEADS // LANES, LANES)
    kv_tail = (PAGE_SIZE, N_HEADS, HEAD_DIM)
    grid_spec = pltpu.PrefetchScalarGridSpec(
        num_scalar_prefetch=1, grid=(dec_b, n_pages // per_step),
        in_specs=[_tok_spec(), _tok_spec(), _tok_spec(), pl.BlockSpec((1, 1, LANES), lambda b, j, pt: (b, 0, 0))]
        + [_page_spec(kv_tail, layer, p, per_step) for p in range(per_step)] * 2
        + [_page_spec(lf_flat.shape[2:], layer, p, per_step) for p in range(per_step)],
        out_specs=_tok_spec(),
        scratch_shapes=[pltpu.VMEM((N_HEADS, 1), F32), pltpu.VMEM((N_HEADS, 1), F32),
                        pltpu.VMEM((N_HEADS, HEAD_DIM), F32), pltpu.VMEM((SUBLANES, LANES), F32)])
    return pl.pallas_call(
        functools.partial(_fox_dec_kernel, per_step=per_step), grid_spec=grid_spec,
        out_shape=jax.ShapeDtypeStruct((dec_b, N_HEADS, HEAD_DIM), F32),
        compiler_params=_cparams("parallel", "arbitrary"), name="fox_dec_attn",
    )(page_table, q3, k_new3, v_new3, lf_new.reshape(dec_b, 1, LANES),
      *([cache_k] * per_step), *([cache_v] * per_step), *([lf_flat] * per_step))


def _pad_lanes(w):
    return jnp.pad(w, ((0, 0), (0, LANES - w.shape[1])))


def kernel(x_prompt, x_sample, cache_k_moba, cache_v_moba, cache_k_fox, cache_v_fox, cache_lf_fox, page_table, norm_mixer, norm_ffn, moba_w_qkv, moba_w_o, fox_w_in, fox_b_f, fox_w_o, ffn_w_gu, ffn_w_down, moe_w_router, moe_w_gu, moe_w_down, norm_final):
    batch, seq, d = x_prompt.shape
    dec_b, dec_seq, _ = x_sample.shape
    depth = norm_mixer.shape[0]
    past_len = page_table.shape[1] * PAGE_SIZE
    assert d == ATT_DIM and dec_seq == 1 and past_len % MOBA_BLOCK == 0
    assert cache_k_moba.shape[2:] == (PAGE_SIZE, N_HEADS, HEAD_DIM)

    xp = x_prompt.reshape(batch * seq, d)
    xs = x_sample.reshape(dec_b, d)
    pos_p = jnp.arange(seq, dtype=jnp.int32)
    pos_s = jnp.full((dec_b,), past_len, jnp.int32)
    outs = {name: [] for name in ("km_p", "vm_p", "kf_p", "vf_p", "lf_p", "km_s", "vm_s", "kf_s", "vf_s", "lf_s")}
    heads3 = lambda a: a.reshape(dec_b, N_HEADS, HEAD_DIM)

    for i in range(depth):
        j = i // 2
        last = i == depth - 1
        if i % 2 == 0:
            w_qkv = moba_w_qkv[j].astype(BF16)
            w_o = moba_w_o[j].astype(BF16)
            pp = _in_proj(xp, norm_mixer[i], w_qkv, prompt=True, rope_pos=pos_p, kmean=True)
            ps = _in_proj(xs, norm_mixer[i], w_qkv, prompt=False, rope_pos=pos_s)
            op = _moba_prompt_attn(pp["q"], pp["kb"], pp["vb"], pp["km"], batch, seq)
            q3 = heads3(ps["q"])
            sel = _moba_dec_select(q3, cache_k_moba, page_table, j)
            page_slot = (sel[:, :, :MOBA_TOPK, None] * PAGES_PER_BLOCK
                         + jnp.arange(PAGES_PER_BLOCK, dtype=jnp.int32)).reshape(dec_b, -1)
            sel_pages = jnp.take_along_axis(page_table, page_slot, axis=1).reshape(-1)
            os_ = _moba_dec_attn(q3, heads3(ps["k"]), heads3(ps["v"]), cache_k_moba, cache_v_moba, sel_pages, j)
            outs["km_p"].append(pp["k"]); outs["vm_p"].append(pp["v"])
            outs["km_s"].append(ps["k"]); outs["vm_s"].append(ps["v"])
            wgu, wd = ffn_w_gu[j].astype(BF16), ffn_w_down[j].astype(BF16)
            xp = _dense_ffn(xp, op, w_o, norm_ffn[i], wgu, wd, norm_final, last)
            xs = _dense_ffn(xs, os_.reshape(dec_b, ATT_DIM), w_o, norm_ffn[i], wgu, wd, norm_final, last)
        else:
            w_in = fox_w_in[j]
            w_qkv = w_in[:, :3 * ATT_DIM].astype(BF16)
            w_f = _pad_lanes(w_in[:, 3 * ATT_DIM:]).astype(BF16)
            b_f = _pad_lanes(fox_b_f[j].reshape(1, N_HEADS))
            w_o = fox_w_o[j].astype(BF16)
            pp = _in_proj(xp, norm_mixer[i], w_qkv, prompt=True, gate_w=w_f, gate_b=b_f, q_dtype=BF16)
            ps = _in_proj(xs, norm_mixer[i], w_qkv, prompt=False, gate_w=w_f, gate_b=b_f)
            op = _fox_prompt_attn(pp["q"], pp["kb"], pp["vb"], pp["lf"], batch, seq)
            os_ = _fox_dec_attn(heads3(ps["q"]), heads3(ps["k"]), heads3(ps["v"]), ps["lf"],
                                cache_k_fox, cache_v_fox, cache_lf_fox, page_table, j)
            outs["kf_p"].append(pp["k"]); outs["vf_p"].append(pp["v"]); outs["lf_p"].append(pp["lf"][:, :N_HEADS])
            outs["kf_s"].append(ps["k"]); outs["vf_s"].append(ps["v"]); outs["lf_s"].append(ps["lf"][:, :N_HEADS])
            wr = _pad_lanes(moe_w_router[j])
            wgu, wd = moe_w_gu[j].astype(BF16), moe_w_down[j].astype(BF16)
            xp = _moe_ffn(xp, op, w_o, norm_ffn[i], wr, wgu, wd, norm_final, last)
            xs = _moe_ffn(xs, os_.reshape(dec_b, ATT_DIM), w_o, norm_ffn[i], wr, wgu, wd, norm_final, last)

    def rows(name, b, t, tail):
        return jnp.stack(outs[name]).reshape((len(outs[name]), b, t) + tail)

    kv_tail = (N_HEADS, HEAD_DIM)
    return (xp.reshape(batch, seq, d), xs.reshape(dec_b, dec_seq, d),
            rows("km_p", batch, seq, kv_tail), rows("vm_p", batch, seq, kv_tail),
            rows("kf_p", batch, seq, kv_tail), rows("vf_p", batch, seq, kv_tail),
            rows("lf_p", batch, seq, (N_HEADS,)),
            rows("km_s", dec_b, dec_seq, kv_tail), rows("vm_s", dec_b, dec_seq, kv_tail),
            rows("kf_s", dec_b, dec_seq, kv_tail), rows("vf_s", dec_b, dec_seq, kv_tail),
            rows("lf_s", dec_b, dec_seq, (N_HEADS,)))
```

```python
import functools
import math

import jax
import jax.numpy as jnp
from jax import lax
from jax.experimental import pallas as pl
from jax.experimental.pallas import tpu as pltpu

F32 = jnp.float32
BF16 = jnp.bfloat16
HIGHEST = lax.Precision.HIGHEST
NEG_INF = float("-inf")

N_HEADS = 8
HEAD_DIM = 128
ATT_DIM = N_HEADS * HEAD_DIM
ROPE_DIM = HEAD_DIM // 4
ROPE_THETA = 500000.0
MOBA_BLOCK = 256
MOBA_TOPK = 3
PAGE_SIZE = 128
MOE_TOPK = 2
MOE_ROW_BLOCK = 256
RMS_EPS = 1e-6
ATT_SCALE = 1.0 / math.sqrt(HEAD_DIM)

LANES = 128
SUBLANES = 8
V7X_VMEM_LIMIT_BYTES = 56 << 20
ROW_TILE = 512
FF_TILE = 256
ATT_TILE = 256
DEC_PAGES_PER_STEP = 8
SELECT_PAGES_PER_STEP = 16
PAGES_PER_BLOCK = MOBA_BLOCK // PAGE_SIZE

assert N_HEADS == SUBLANES and HEAD_DIM == LANES


def _cparams(*sem):
    return pltpu.CompilerParams(dimension_semantics=sem, vmem_limit_bytes=V7X_VMEM_LIMIT_BYTES)


def _row_tile(n):
    return ROW_TILE if n % ROW_TILE == 0 else n


def _rms_norm(x, g):
    return x * lax.rsqrt(jnp.mean(x * x, axis=-1, keepdims=True) + RMS_EPS) * g


def _log_sigmoid(z):
    return -(jnp.maximum(-z, 0.0) + jnp.log1p(jnp.exp(-jnp.abs(z))))


def _dot_nt(a, b, **kw):
    return lax.dot_general(a, b, (((1,), (1,)), ((), ())), preferred_element_type=F32, **kw)


def _head(h):
    return slice(h * HEAD_DIM, (h + 1) * HEAD_DIM)


def _in_proj_kernel(*refs, rope, gate, prompt, kmean):
    it = iter(refs)
    x_ref, g_ref, w_ref = next(it), next(it), next(it)
    if rope:
        cos_ref, sa_ref, sb_ref = next(it), next(it), next(it)
    if gate:
        wf_ref, bf_ref = next(it), next(it)
    q_ref, k_ref, v_ref = next(it), next(it), next(it)
    if prompt:
        kb_ref, vb_ref = next(it), next(it)
    if kmean:
        km_ref = next(it)
    tm = x_ref.shape[0]
    hn = _rms_norm(x_ref[...], g_ref[...]).astype(BF16)
    half = ROPE_DIM // 2
    for c in range(3):
        y = jnp.dot(hn, w_ref[:, c * ATT_DIM:(c + 1) * ATT_DIM], preferred_element_type=F32)
        for h in range(N_HEADS):
            yh = y[:, _head(h)]
            if rope and c < 2:
                yh = (yh * cos_ref[...] + pltpu.roll(yh, HEAD_DIM - half, 1) * sa_ref[...]
                      + pltpu.roll(yh, half, 1) * sb_ref[...])
            per_head_rows = pl.ds(h, tm, stride=N_HEADS)
            if c == 0:
                if prompt:
                    q_ref[:, _head(h)] = yh.astype(q_ref.dtype)
                else:
                    q_ref[per_head_rows, :] = yh
            else:
                (k_ref if c == 1 else v_ref)[per_head_rows, :] = yh
                if prompt:
                    (kb_ref if c == 1 else vb_ref)[:, _head(h)] = yh.astype(BF16)
                if kmean and c == 1:
                    for blk in range(tm // MOBA_BLOCK):
                        rows = yh[blk * MOBA_BLOCK:(blk + 1) * MOBA_BLOCK]
                        km_ref[blk, :, _head(h)] = jnp.mean(rows, axis=0, keepdims=True)
    if gate:
        lf_ref = next(it)
        z = jnp.dot(hn, wf_ref[...], preferred_element_type=F32) + bf_ref[...]
        lf_ref[...] = _log_sigmoid(z)


def _rope_tables(pos):
    half = ROPE_DIM // 2
    inv = jnp.power(jnp.float32(ROPE_THETA), -jnp.arange(half, dtype=F32) / half)
    ang = pos.astype(F32)[:, None] * inv[None, :]
    cos, sin = jnp.cos(ang), jnp.sin(ang)
    t = pos.shape[0]
    rest = HEAD_DIM - ROPE_DIM
    cos_t = jnp.concatenate([cos, cos, jnp.ones((t, rest), F32)], axis=1)
    sa_t = jnp.concatenate([-sin, jnp.zeros((t, half + rest), F32)], axis=1)
    sb_t = jnp.concatenate([jnp.zeros((t, half), F32), sin, jnp.zeros((t, rest), F32)], axis=1)
    return cos_t, sa_t, sb_t


def _in_proj(x2d, g, w_bf, *, prompt, rope_pos=None, kmean=False, gate_w=None, gate_b=None, q_dtype=F32):
    n, d = x2d.shape
    tm = _row_tile(n)
    rope, gate = rope_pos is not None, gate_w is not None
    assert not kmean or (prompt and tm % MOBA_BLOCK == 0)
    args = [x2d, g.reshape(1, d), w_bf]
    in_specs = [pl.BlockSpec((tm, d), lambda i: (i, 0)), _resident((1, d)), _resident((d, 3 * ATT_DIM))]
    if rope:
        nt = rope_pos.shape[0] // tm
        args += list(_rope_tables(rope_pos))
        in_specs += [pl.BlockSpec((tm, HEAD_DIM), lambda i: (i % nt, 0))] * 3
    if gate:
        args += [gate_w, gate_b]
        in_specs += [_resident((d, LANES)), _resident((1, LANES))]

    rows2d = (jax.ShapeDtypeStruct((n, ATT_DIM), F32), pl.BlockSpec((tm, ATT_DIM), lambda i: (i, 0)))
    head_rows = (jax.ShapeDtypeStruct((n * N_HEADS, HEAD_DIM), F32),
                 pl.BlockSpec((tm * N_HEADS, HEAD_DIM), lambda i: (i, 0)))
    names = ["q", "k", "v"]
    outs = [(jax.ShapeDtypeStruct((n, ATT_DIM), q_dtype), rows2d[1]) if prompt else head_rows, head_rows, head_rows]
    if prompt:
        names += ["kb", "vb"]
        outs += [(jax.ShapeDtypeStruct((n, ATT_DIM), BF16), rows2d[1])] * 2
    if kmean:
        names.append("km")
        nbt = tm // MOBA_BLOCK
        outs.append((jax.ShapeDtypeStruct((n // MOBA_BLOCK, 1, ATT_DIM), F32),
                     pl.BlockSpec((nbt, 1, ATT_DIM), lambda i: (i, 0, 0))))
    if gate:
        names.append("lf")
        outs.append((jax.ShapeDtypeStruct((n, LANES), F32), pl.BlockSpec((tm, LANES), lambda i: (i, 0))))
    res = pl.pallas_call(
        functools.partial(_in_proj_kernel, rope=rope, gate=gate, prompt=prompt, kmean=kmean),
        grid=(n // tm,), in_specs=in_specs,
        out_specs=[o[1] for o in outs], out_shape=[o[0] for o in outs],
        compiler_params=_cparams("parallel"), name="in_proj")(*args)
    return dict(zip(names, res))


def _resident(shape):
    return pl.BlockSpec(shape, lambda *_: (0,) * len(shape), pipeline_mode=pl.Buffered(1))


def _mixer_residual(x_ref, o_ref, wo_ref):
    return x_ref[...] + jnp.dot(o_ref[...].astype(BF16), wo_ref[...], preferred_element_type=F32)

def _swiglu_into(act_ref, xb, wgu_ref, d_ff, after_tile=None):
    for t in range(d_ff // FF_TILE):
        lo, hi = t * FF_TILE, (t + 1) * FF_TILE
        g = jnp.dot(xb, wgu_ref[:, lo:hi], preferred_element_type=F32)
        u = jnp.dot(xb, wgu_ref[:, d_ff + lo:d_ff + hi], preferred_element_type=F32)
        act_ref[:, lo:hi] = (g * jax.nn.sigmoid(g) * u).astype(BF16)
        if after_tile is not None:
            after_tile(t)


def _ffn_kernel(x_ref, o_ref, wo_ref, g_ref, wgu_ref, wd_ref, gfin_ref, y_ref, act_ref, *, d_ff, final_norm):
    x = _mixer_residual(x_ref, o_ref, wo_ref)
    hn = _rms_norm(x, g_ref[...]).astype(BF16)
    _swiglu_into(act_ref, hn, wgu_ref, d_ff)
    y = x + jnp.dot(act_ref[...], wd_ref[...], preferred_element_type=F32)
    if final_norm:
        y = _rms_norm(y, gfin_ref[...])
    y_ref[...] = y


def _dense_ffn(x2d, o, wo_bf, g, wgu_bf, wd_bf, g_final, final_norm):
    n, d = x2d.shape
    d_ff = wd_bf.shape[0]
    assert d_ff % FF_TILE == 0
    tm = _row_tile(n)
    rows = lambda width: pl.BlockSpec((tm, width), lambda i: (i, 0))
    return pl.pallas_call(
        functools.partial(_ffn_kernel, d_ff=d_ff, final_norm=final_norm),
        grid=(n // tm,),
        in_specs=[rows(d), rows(ATT_DIM), _resident((ATT_DIM, d)), _resident((1, d)),
                  _resident((d, 2 * d_ff)), _resident((d_ff, d)), _resident((1, d))],
        out_specs=rows(d),
        out_shape=jax.ShapeDtypeStruct((n, d), F32),
        scratch_shapes=[pltpu.VMEM((tm, d_ff), BF16)],
        compiler_params=_cparams("parallel"), name="dense_ffn",
    )(x2d, o, wo_bf, g.reshape(1, d), wgu_bf, wd_bf, g_final.reshape(1, d))


def _router_kernel(x_ref, o_ref, wo_ref, g_ref, wr_ref, xm_ref, hn_ref, route_ref, *, n_experts):
    x = _mixer_residual(x_ref, o_ref, wo_ref)
    xm_ref[...] = x
    hn = _rms_norm(x, g_ref[...])
    hn_ref[...] = hn
    logits = jnp.dot(hn.astype(BF16), wr_ref[...], preferred_element_type=F32)
    lane = lax.broadcasted_iota(jnp.int32, logits.shape, 1)
    logits = jnp.where(lane < n_experts, logits, NEG_INF)
    m1 = jnp.max(logits, axis=1, keepdims=True)
    i1 = jnp.min(jnp.where(logits == m1, lane, LANES), axis=1, keepdims=True)
    rest = jnp.where(lane == i1, NEG_INF, logits)
    m2 = jnp.max(rest, axis=1, keepdims=True)
    i2 = jnp.min(jnp.where(rest == m2, lane, LANES), axis=1, keepdims=True)
    e2 = jnp.exp(m2 - m1)
    den = 1.0 + e2
    route_ref[...] = jnp.where(lane == 0, i1.astype(F32),
                     jnp.where(lane == 1, i2.astype(F32),
                     jnp.where(lane == 2, 1.0 / den,
                     jnp.where(lane == 3, e2 / den, 0.0))))


def _router(x2d, o, wo_bf, g, wr_pad, n_experts):
    n, d = x2d.shape
    tm = _row_tile(n)
    rows = lambda width: pl.BlockSpec((tm, width), lambda i: (i, 0))
    return pl.pallas_call(
        functools.partial(_router_kernel, n_experts=n_experts), grid=(n // tm,),
        in_specs=[rows(d), rows(ATT_DIM), _resident((ATT_DIM, d)), _resident((1, d)), _resident((d, LANES))],
        out_specs=[rows(d), rows(d), rows(LANES)],
        out_shape=[jax.ShapeDtypeStruct((n, d), F32), jax.ShapeDtypeStruct((n, d), F32),
                   jax.ShapeDtypeStruct((n, LANES), F32)],
        compiler_params=_cparams("parallel"), name="moe_router",
    )(x2d, o, wo_bf, g.reshape(1, d), wr_pad.astype(BF16))


def _row_copy(src_hbm, src_row, dst, dst_row, sem):
    return pltpu.make_async_copy(src_hbm.at[pl.ds(src_row, 1)], dst.at[pl.ds(dst_row, 1)], sem)


def _expert_kernel(be_ref, src_ref, nxt_ref, dprev_ref, dcur_ref, hn_hbm, wgu_ref, wd_ref, y_hbm,
                   xbuf, ybuf, act_ref, gsem, ssem, *, d_ff, rows):
    del be_ref
    i = pl.program_id(0)
    slot = lax.rem(i, 2)
    other = 1 - slot

    def gather(idx_ref, r, s):
        return _row_copy(hn_hbm, idx_ref[0, 0, r], xbuf.at[s], r, gsem.at[s])

    def scatter(dst_ref, r, s):
        return pltpu.make_async_copy(ybuf.at[s, pl.ds(r, 1)], y_hbm.at[pl.ds(dst_ref[0, 0, r], 1)], ssem.at[s])

    def wait_gather(s):
        for r in range(rows):
            _row_copy(hn_hbm, 0, xbuf.at[s], r, gsem.at[s]).wait()

    def wait_scatter(s):
        for r in range(rows):
            pltpu.make_async_copy(ybuf.at[s, pl.ds(r, 1)], y_hbm.at[pl.ds(0, 1)], ssem.at[s]).wait()

    def start_all(make):
        def body(r, c):
            make(r).start()
            return c
        lax.fori_loop(0, rows, body, 0)

    @pl.when(i == 0)
    def _():
        start_all(lambda r: gather(src_ref, r, 0))
        ybuf[1] = jnp.zeros(ybuf.shape[1:], F32)

    @pl.when(i > 0)
    def _():
        wait_scatter(slot)

    wait_gather(slot)
    xb = xbuf[slot].astype(BF16)
    n_tiles = d_ff // FF_TILE
    per_tile = -(-rows // n_tiles)

    def side_dmas(t):
        for r in range(t * per_tile, min(rows, (t + 1) * per_tile)):
            gather(nxt_ref, r, other).start()
            scatter(dprev_ref, r, other).start()

    _swiglu_into(act_ref, xb, wgu_ref.at[0], d_ff, after_tile=side_dmas)
    ybuf[slot] = jnp.dot(act_ref[...], wd_ref[0], preferred_element_type=F32)

    @pl.when(i == pl.num_programs(0) - 1)
    def _():
        wait_gather(other)
        wait_scatter(other)
        start_all(lambda r: scatter(dcur_ref, r, slot))
        wait_scatter(slot)


def _expert_ffn(hn, row_src, row_dst, blk_e, wgu_bf, wd_bf, rows):
    d = hn.shape[1]
    n_rows = row_src.shape[0]
    nb = n_rows // rows
    d_ff = wd_bf.shape[1]
    assert d_ff % FF_TILE == 0

    def idx_spec(ahead):
        return pl.BlockSpec((1, 1, rows), lambda i, be: (jnp.minimum(i + ahead, nb - 1), 0, 0),
                            memory_space=pltpu.SMEM)

    def dst_spec(ahead):
        return pl.BlockSpec((1, 1, rows), lambda i, be: (i + ahead, 0, 0), memory_space=pltpu.SMEM)

    grid_spec = pltpu.PrefetchScalarGridSpec(
        num_scalar_prefetch=1, grid=(nb,),
        in_specs=[idx_spec(0), idx_spec(1), dst_spec(0), dst_spec(1), pl.BlockSpec(memory_space=pl.ANY),
                  pl.BlockSpec((1, d, 2 * d_ff), lambda i, be: (be[i], 0, 0), pipeline_mode=pl.Buffered(1)),
                  pl.BlockSpec((1, d_ff, d), lambda i, be: (be[i], 0, 0), pipeline_mode=pl.Buffered(1))],
        out_specs=pl.BlockSpec(memory_space=pl.ANY),
        scratch_shapes=[pltpu.VMEM((2, rows, d), F32), pltpu.VMEM((2, rows, d), F32),
                        pltpu.VMEM((rows, d_ff), BF16),
                        pltpu.SemaphoreType.DMA((2,)), pltpu.SemaphoreType.DMA((2,))])
    src3 = row_src.reshape(nb, 1, rows)
    spare = n_rows + jnp.arange(rows, dtype=jnp.int32)
    dst3 = jnp.concatenate([spare, row_dst]).reshape(nb + 1, 1, rows)
    return pl.pallas_call(
        functools.partial(_expert_kernel, d_ff=d_ff, rows=rows), grid_spec=grid_spec,
        out_shape=jax.ShapeDtypeStruct((n_rows + rows, d), F32),
        compiler_params=_cparams("arbitrary"), name="moe_experts",
    )(blk_e, src3, src3, dst3, dst3, hn, wgu_bf, wd_bf)


def _combine_kernel(x_ref, y0_ref, y1_ref, route_ref, gfin_ref, o_ref, *, final_norm):
    route = route_ref[...]
    y = x_ref[...] + (y0_ref[...] * route[:, 2:3] + y1_ref[...] * route[:, 3:4])
    if final_norm:
        y = _rms_norm(y, gfin_ref[...])
    o_ref[...] = y


def _moe_combine(x2d, route, y, g_final, final_norm):
    n, d = x2d.shape
    tm = _row_tile(n)
    nb = n // tm
    return pl.pallas_call(
        functools.partial(_combine_kernel, final_norm=final_norm), grid=(nb,),
        in_specs=[pl.BlockSpec((tm, d), lambda i: (i, 0)),
                  pl.BlockSpec((tm, d), lambda i: (i, 0)),
                  pl.BlockSpec((tm, d), lambda i: (nb + i, 0)),
                  pl.BlockSpec((tm, LANES), lambda i: (i, 0)),
                  _resident((1, d))],
        out_specs=pl.BlockSpec((tm, d), lambda i: (i, 0)),
        out_shape=jax.ShapeDtypeStruct((n, d), F32),
        compiler_params=_cparams("parallel"), name="moe_combine",
    )(x2d, y, y, route, g_final.reshape(1, d))


def _moe_plan(top_idx, n_experts):
    n = top_idx.shape[0]
    n_asg = n * MOE_TOPK
    blk = min(MOE_ROW_BLOCK, n_asg)
    n_rows = -(-(n_asg + n_experts * (blk - 1)) // blk) * blk
    flat_e = top_idx.reshape(-1)
    onehot = (flat_e[:, None] == jnp.arange(n_experts, dtype=jnp.int32)[None, :]).astype(jnp.int32)
    csum = jnp.cumsum(onehot, axis=0)
    counts = csum[-1]
    rank = jnp.take_along_axis(csum, flat_e[:, None], axis=1)[:, 0] - 1
    padded = (counts + blk - 1) // blk * blk
    pad_end = jnp.cumsum(padded)
    pad_start = pad_end - padded
    row = (pad_start[flat_e] + rank).astype(jnp.int32)
    row_asg = jnp.full((n_rows,), -1, jnp.int32).at[row].set(jnp.arange(n_asg, dtype=jnp.int32))
    is_pad = row_asg < 0
    tok = jnp.where(is_pad, 0, row_asg // MOE_TOPK)
    pad_rank = jnp.cumsum(is_pad.astype(jnp.int32)) - 1
    row_dst = jnp.where(is_pad, n_asg + pad_rank, (row_asg % MOE_TOPK) * n + tok).astype(jnp.int32)
    blk_start = jnp.arange(n_rows // blk, dtype=jnp.int32) * blk
    blk_e = jnp.minimum(jnp.searchsorted(pad_end, blk_start, side="right"), n_experts - 1).astype(jnp.int32)
    return tok.astype(jnp.int32), row_dst, blk_e, blk


def _moe_ffn(x2d, o, wo_bf, g, wr_pad, wgu_bf, wd_bf, g_final, final_norm):
    n_experts = wgu_bf.shape[0]
    xm, hn, route = _router(x2d, o, wo_bf, g, wr_pad, n_experts)
    top_idx = route[:, :MOE_TOPK].astype(jnp.int32)
    row_src, row_dst, blk_e, blk = _moe_plan(top_idx, n_experts)
    y = _expert_ffn(hn, row_src, row_dst, blk_e, wgu_bf, wd_bf, blk)
    return _moe_combine(xm, route, y, g_final, final_norm)


def _transpose_v(vt_ref, vb_ref, n_tiles):
    for n in range(n_tiles):
        rows = slice(n * ATT_TILE, (n + 1) * ATT_TILE)
        vt_ref[:, rows] = vb_ref[rows, :].astype(F32).T.astype(BF16)


def _causal_bias():
    key = lax.broadcasted_iota(jnp.int32, (ATT_TILE, ATT_TILE), 0)
    query = lax.broadcasted_iota(jnp.int32, (ATT_TILE, ATT_TILE), 1)
    return jnp.where(key <= query, 0.0, NEG_INF)


def _attend_t(qi, qb_t, kb_ref, vt_ref, causal, past_bias=None, bias=None):
    n_keys = (qi + 1) * ATT_TILE
    s_all = jnp.dot(kb_ref[0:n_keys, :], qb_t, preferred_element_type=F32)
    tiles = []
    for n in range(qi + 1):
        s = s_all[n * ATT_TILE:(n + 1) * ATT_TILE] * ATT_SCALE
        if bias is not None:
            s = s - bias(n)
        if n == qi:
            s = s + causal
        elif past_bias is not None:
            s = s + past_bias(n)
        tiles.append(s)
    m = functools.reduce(jnp.maximum, [jnp.max(s, axis=0, keepdims=True) for s in tiles])
    probs = [jnp.exp(s - m) for s in tiles]
    l = functools.reduce(jnp.add, [jnp.sum(p, axis=0, keepdims=True) for p in probs])
    p_all = jnp.concatenate([p.astype(BF16) for p in probs], axis=0)
    return jnp.dot(vt_ref[:, 0:n_keys], p_all, preferred_element_type=F32) / l


def _moba_attn_kernel(q_ref, kb_ref, vb_ref, km_ref, o_ref, vt_ref, kmp_ref, *, nb):
    _transpose_v(vt_ref, vb_ref, nb)
    nbp = kmp_ref.shape[0]
    kmp_ref[...] = jnp.zeros(kmp_ref.shape, F32)
    kmp_ref[0:nb, :] = km_ref[...].reshape(nb, HEAD_DIM)
    causal = _causal_bias()
    k_top = min(MOBA_TOPK, nb)
    for qi in range(nb):
        rows = slice(qi * ATT_TILE, (qi + 1) * ATT_TILE)
        q_t = q_ref[rows, :].T
        past_bias = None
        if qi > k_top:
            gate = jnp.dot(kmp_ref[...], q_t, preferred_element_type=F32, precision=HIGHEST)
            blk = lax.broadcasted_iota(jnp.int32, gate.shape, 0)
            g = jnp.where(blk < qi, gate, NEG_INF)
            sel = jnp.full(g.shape, NEG_INF, F32)
            for _ in range(k_top):
                m = jnp.max(g, axis=0, keepdims=True)
                idx = jnp.min(jnp.where(g == m, blk, nbp), axis=0, keepdims=True)
                pick = blk == idx
                sel = jnp.where(pick, 0.0, sel)
                g = jnp.where(pick, NEG_INF, g)
            past_bias = lambda n, sel=sel: sel[n:n + 1, :]
        acc_t = _attend_t(qi, q_t.astype(BF16), kb_ref, vt_ref, causal, past_bias)
        o_ref[rows, :] = acc_t.T.astype(o_ref.dtype)


def _moba_prompt_attn(q, kb, vb, km, batch, seq):
    assert seq % MOBA_BLOCK == 0 and ATT_TILE == MOBA_BLOCK
    nb = seq // MOBA_BLOCK
    nbp = -(-nb // SUBLANES) * SUBLANES
    spec = pl.BlockSpec((seq, HEAD_DIM), lambda b, h: (b, h))
    return pl.pallas_call(
        functools.partial(_moba_attn_kernel, nb=nb), grid=(batch, N_HEADS),
        in_specs=[spec, spec, spec, pl.BlockSpec((nb, 1, HEAD_DIM), lambda b, h: (b, 0, h))],
        out_specs=spec,
        out_shape=jax.ShapeDtypeStruct(q.shape, BF16),
        scratch_shapes=[pltpu.VMEM((HEAD_DIM, seq), BF16), pltpu.VMEM((nbp, HEAD_DIM), F32)],
        compiler_params=_cparams("parallel", "parallel"), name="moba_prompt_attn")(q, kb, vb, km)


def _fox_attn_kernel(q_ref, kb_ref, vb_ref, lf_ref, o_ref, vt_ref, call_ref, cb_ref, *, nq):
    head = pl.program_id(1)
    tiles = [slice(n * ATT_TILE, (n + 1) * ATT_TILE) for n in range(nq)]

    @pl.when(head == 0)
    def _():
        tri = (lax.broadcasted_iota(jnp.int32, (ATT_TILE, ATT_TILE), 0)
               >= lax.broadcasted_iota(jnp.int32, (ATT_TILE, ATT_TILE), 1)).astype(F32)
        run = jnp.zeros((1, LANES), F32)
        for rows in tiles:
            c = jnp.dot(tri, lf_ref[rows, :], preferred_element_type=F32, precision=HIGHEST) + run
            call_ref[rows, :] = c
            run = c[ATT_TILE - 1:ATT_TILE, :]

    _transpose_v(vt_ref, vb_ref, nq)
    lane = lax.broadcasted_iota(jnp.int32, (ATT_TILE, LANES), 1)
    for rows in tiles:
        col = jnp.sum(jnp.where(lane == head, call_ref[rows, :], 0.0), axis=1, keepdims=True)
        cb_ref[rows, :] = jnp.broadcast_to(col, (ATT_TILE, LANES))

    def bias(n):
        return jnp.concatenate([cb_ref[tiles[n], :]] * (ATT_TILE // LANES), axis=1)

    causal = _causal_bias()
    for qi in range(nq):
        qb_t = q_ref[tiles[qi], :].astype(F32).T.astype(BF16)
        acc_t = _attend_t(qi, qb_t, kb_ref, vt_ref, causal, bias=bias)
        o_ref[tiles[qi], :] = acc_t.T.astype(o_ref.dtype)


def _fox_prompt_attn(q_bf, kb, vb, lf, batch, seq):
    assert seq % ATT_TILE == 0
    nq = seq // ATT_TILE
    spec = pl.BlockSpec((seq, HEAD_DIM), lambda b, h: (b, h))
    return pl.pallas_call(
        functools.partial(_fox_attn_kernel, nq=nq), grid=(batch, N_HEADS),
        in_specs=[spec, spec, spec, pl.BlockSpec((seq, LANES), lambda b, h: (b, 0))],
        out_specs=spec,
        out_shape=jax.ShapeDtypeStruct(q_bf.shape, BF16),
        scratch_shapes=[pltpu.VMEM((HEAD_DIM, seq), BF16), pltpu.VMEM((seq, LANES), F32),
                        pltpu.VMEM((seq, LANES), F32)],
        compiler_params=_cparams("parallel", "arbitrary"), name="fox_prompt_attn")(q_bf, kb, vb, lf)


def _page_spec(tail, layer, p, per_step):
    zeros = (0,) * len(tail)
    return pl.BlockSpec((1, 1) + tail, lambda b, j, pt: (layer, pt[b, j * per_step + p]) + zeros)


def _tok_spec():
    return pl.BlockSpec((1, N_HEADS, HEAD_DIM), lambda b, *_: (b, 0, 0))


def _moba_dec_stats_kernel(pt_ref, q_ref, *refs, per_step, n_blk):
    del pt_ref
    pages, sel_ref, gate_ref = refs[:per_step], refs[per_step], refs[per_step + 1]
    j = pl.program_id(1)
    lane = lax.broadcasted_iota(jnp.int32, (N_HEADS, LANES), 1)

    @pl.when(j == 0)
    def _():
        gate_ref[...] = jnp.full(gate_ref.shape, NEG_INF, F32)

    q = q_ref[0]
    blk_per_step = per_step // PAGES_PER_BLOCK
    for t in range(blk_per_step):
        ksum = jnp.sum(pages[PAGES_PER_BLOCK * t][0, 0], axis=0)
        for u in range(1, PAGES_PER_BLOCK):
            ksum = ksum + jnp.sum(pages[PAGES_PER_BLOCK * t + u][0, 0], axis=0)
        g = jnp.sum(ksum * q, axis=1, keepdims=True) * (1.0 / MOBA_BLOCK)
        gate_ref[...] = jnp.where(lane == j * blk_per_step + t, g, gate_ref[...])

    @pl.when(j == pl.num_programs(1) - 1)
    def _():
        g = gate_ref[...]
        out = jnp.zeros((N_HEADS, LANES), jnp.int32)
        for r in range(MOBA_TOPK):
            m = jnp.max(g, axis=1, keepdims=True)
            idx = jnp.min(jnp.where(g == m, lane, LANES), axis=1, keepdims=True)
            out = jnp.where(lane == r, idx, out)
            g = jnp.where(lane == idx, NEG_INF, g)
        sel_ref[0] = out


def _moba_dec_select(q3, cache_k, page_table, layer):
    dec_b, n_pages = page_table.shape
    per_step = math.gcd(n_pages, SELECT_PAGES_PER_STEP)
    assert per_step % PAGES_PER_BLOCK == 0
    n_blk = n_pages // PAGES_PER_BLOCK
    assert MOBA_TOPK <= n_blk <= LANES
    grid_spec = pltpu.PrefetchScalarGridSpec(
        num_scalar_prefetch=1, grid=(dec_b, n_pages // per_step),
        in_specs=[_tok_spec()]
        + [_page_spec((PAGE_SIZE, N_HEADS, HEAD_DIM), layer, p, per_step) for p in range(per_step)],
        out_specs=pl.BlockSpec((1, N_HEADS, LANES), lambda b, j, pt: (b, 0, 0)),
        scratch_shapes=[pltpu.VMEM((N_HEADS, LANES), F32)])
    return pl.pallas_call(
        functools.partial(_moba_dec_stats_kernel, per_step=per_step, n_blk=n_blk), grid_spec=grid_spec,
        out_shape=jax.ShapeDtypeStruct((dec_b, N_HEADS, LANES), jnp.int32),
        compiler_params=_cparams("parallel", "arbitrary"), name="moba_dec_select",
    )(page_table, q3, *([cache_k] * per_step))


def _moba_dec_attn_kernel(pg_ref, q_ref, kn_ref, vn_ref, ck_hbm, cv_hbm, o_ref, kbuf, vbuf, sems, *, layer, n_sel):
    b = pl.program_id(0)
    last = pl.num_programs(0) - 1
    slot = lax.rem(b, 2)

    def copies(seq, s, h, t):
        page = pg_ref[(seq * N_HEADS + h) * n_sel + t]
        rows = pl.ds(t * PAGE_SIZE, PAGE_SIZE)
        return (pltpu.make_async_copy(ck_hbm.at[layer, page, :, h, :], kbuf.at[s, h, rows, :], sems.at[s, 0]),
                pltpu.make_async_copy(cv_hbm.at[layer, page, :, h, :], vbuf.at[s, h, rows, :], sems.at[s, 1]))

    def for_all(seq, s, act):
        for h in range(N_HEADS):
            for t in range(n_sel):
                for cp in copies(seq, s, h, t):
                    act(cp)

    @pl.when(b == 0)
    def _():
        for_all(0, 0, lambda cp: cp.start())

    for_all(jnp.minimum(b + 1, last), 1 - slot, lambda cp: cp.start())
    for_all(b, slot, lambda cp: cp.wait())

    for h in range(N_HEADS):
        q = q_ref[0, h:h + 1, :]
        q8 = jnp.broadcast_to(q, (SUBLANES, HEAD_DIM)).astype(BF16)
        s = _dot_nt(q8, kbuf[slot, h].astype(BF16))[0:1] * ATT_SCALE
        s_new = jnp.sum(q * kn_ref[0, h:h + 1, :], axis=1, keepdims=True) * ATT_SCALE
        m = jnp.maximum(jnp.max(s, axis=1, keepdims=True), s_new)
        p = jnp.exp(s - m)
        p_new = jnp.exp(s_new - m)
        l = jnp.sum(p, axis=1, keepdims=True) + p_new
        p8 = jnp.broadcast_to(p, (SUBLANES, p.shape[1])).astype(BF16)
        pv = jnp.dot(p8, vbuf[slot, h].astype(BF16), preferred_element_type=F32)[0:1]
        o_ref[0, h:h + 1, :] = (pv + p_new * vn_ref[0, h:h + 1, :]) / l

    @pl.when(b == last)
    def _():
        for_all(b, 1 - slot, lambda cp: cp.wait())


def _moba_dec_attn(q3, k_new3, v_new3, cache_k, cache_v, sel_pages, layer):
    dec_b = q3.shape[0]
    n_sel = MOBA_TOPK * PAGES_PER_BLOCK
    grid_spec = pltpu.PrefetchScalarGridSpec(
        num_scalar_prefetch=1, grid=(dec_b,),
        in_specs=[_tok_spec(), _tok_spec(), _tok_spec(),
                  pl.BlockSpec(memory_space=pl.ANY), pl.BlockSpec(memory_space=pl.ANY)],
        out_specs=_tok_spec(),
        scratch_shapes=[pltpu.VMEM((2, N_HEADS, n_sel * PAGE_SIZE, HEAD_DIM), F32),
                        pltpu.VMEM((2, N_HEADS, n_sel * PAGE_SIZE, HEAD_DIM), F32),
                        pltpu.SemaphoreType.DMA((2, 2))])
    return pl.pallas_call(
        functools.partial(_moba_dec_attn_kernel, layer=layer, n_sel=n_sel), grid_spec=grid_spec,
        out_shape=jax.ShapeDtypeStruct((dec_b, N_HEADS, HEAD_DIM), F32),
        compiler_params=_cparams("arbitrary"), name="moba_dec_attn",
    )(sel_pages, q3, k_new3, v_new3, cache_k, cache_v)


def _fox_dec_kernel(pt_ref, q_ref, kn_ref, vn_ref, lfn_ref, *refs, per_step):
    del pt_ref
    k_pages, v_pages, lf_pages = refs[:per_step], refs[per_step:2 * per_step], refs[2 * per_step:3 * per_step]
    o_ref, m_ref, l_ref, acc_ref, run_ref = refs[3 * per_step:]
    j = pl.program_id(1)
    flat = PAGE_SIZE * N_HEADS

    @pl.when(j == 0)
    def _():
        m_ref[...] = jnp.full(m_ref.shape, NEG_INF, F32)
        l_ref[...] = jnp.zeros(l_ref.shape, F32)
        acc_ref[...] = jnp.zeros(acc_ref.shape, F32)
        run_ref[...] = jnp.zeros(run_ref.shape, F32)

    q = q_ref[0]
    qb = q.astype(BF16)
    lane = lax.broadcasted_iota(jnp.int32, (SUBLANES, LANES), 1)
    sub = lax.broadcasted_iota(jnp.int32, (SUBLANES, LANES), 0)
    col_head = lax.broadcasted_iota(jnp.int32, (N_HEADS, flat), 1) & (N_HEADS - 1)
    own_head = col_head == lax.broadcasted_iota(jnp.int32, (N_HEADS, flat), 0)

    def update(s, pv_of):
        m_old = m_ref[...]
        m_new = jnp.maximum(m_old, jnp.max(s, axis=1, keepdims=True))
        alpha = jnp.exp(m_old - m_new)
        p = jnp.exp(s - m_new)
        l_ref[...] = alpha * l_ref[...] + jnp.sum(p, axis=1, keepdims=True)
        acc_ref[...] = alpha * acc_ref[...] + pv_of(p)
        m_ref[...] = m_new

    run = run_ref[...]
    biases = []
    for t in range(per_step):
        x = lf_pages[t][0, 0]
        tot = x
        d = N_HEADS
        while d < LANES:
            x = x + jnp.where(lane >= d, pltpu.roll(x, d, 1), 0.0)
            tot = tot + pltpu.roll(tot, d, 1)
            d *= 2
        inc = tot
        d = 1
        while d < SUBLANES:
            inc = inc + jnp.where(sub >= d, pltpu.roll(inc, d, 0), 0.0)
            d *= 2
        c = x + (inc - tot) + run
        run = run + inc[SUBLANES - 1:SUBLANES, :]
        biases += [-c[a:a + 1, :] for a in range(SUBLANES)]
    run_ref[...] = run
    bias = jnp.concatenate(biases, axis=1)

    k2 = jnp.concatenate([r[0, 0].reshape(flat, HEAD_DIM).astype(BF16) for r in k_pages], axis=0)
    v2 = jnp.concatenate([r[0, 0].reshape(flat, HEAD_DIM).astype(BF16) for r in v_pages], axis=0)
    own_head = jnp.concatenate([own_head] * per_step, axis=1)
    s = jnp.where(own_head, _dot_nt(qb, k2) * ATT_SCALE + bias, NEG_INF)
    update(s, lambda p: jnp.dot(p.astype(BF16), v2, preferred_element_type=F32))

    @pl.when(j == pl.num_programs(1) - 1)
    def _():
        c_new = run_ref[...] + lfn_ref[0]
        c_col = jnp.sum(jnp.where(lane == sub, c_new, 0.0), axis=1, keepdims=True)
        s_new = jnp.sum(q * kn_ref[0], axis=1, keepdims=True) * ATT_SCALE - c_col
        update(s_new, lambda p: p * vn_ref[0])
        o_ref[0] = acc_ref[...] / l_ref[...]


def _fox_dec_attn(q3, k_new3, v_new3, lf_new, cache_k, cache_v, cache_lf, page_table, layer):
    dec_b, n_pages = page_table.shape
    per_step = math.gcd(n_pages, DEC_PAGES_PER_STEP)
    lf_flat = cache_lf.reshape(cache_lf.shape[0], cache_lf.shape[1], PAGE_SIZE * N_HEADS // LANES, LANES)
    kv_tail = (PAGE_SIZE, N_HEADS, HEAD_DIM)
    grid_spec = pltpu.PrefetchScalarGridSpec(
        num_scalar_prefetch=1, grid=(dec_b, n_pages // per_step),
        in_specs=[_tok_spec(), _tok_spec(), _tok_spec(), pl.BlockSpec((1, 1, LANES), lambda b, j, pt: (b, 0, 0))]
        + [_page_spec(kv_tail, layer, p, per_step) for p in range(per_step)] * 2
        + [_page_spec(lf_flat.shape[2:], layer, p, per_step) for p in range(per_step)],
        out_specs=_tok_spec(),
        scratch_shapes=[pltpu.VMEM((N_HEADS, 1), F32), pltpu.VMEM((N_HEADS, 1), F32),
                        pltpu.VMEM((N_HEADS, HEAD_DIM), F32), pltpu.VMEM((SUBLANES, LANES), F32)])
    return pl.pallas_call(
        functools.partial(_fox_dec_kernel, per_step=per_step), grid_spec=grid_spec,
        out_shape=jax.ShapeDtypeStruct((dec_b, N_HEADS, HEAD_DIM), F32),
        compiler_params=_cparams("parallel", "arbitrary"), name="fox_dec_attn",
    )(page_table, q3, k_new3, v_new3, lf_new.reshape(dec_b, 1, LANES),
      *([cache_k] * per_step), *([cache_v] * per_step), *([lf_flat] * per_step))


def _pad_lanes(w):
    return jnp.pad(w, ((0, 0), (0, LANES - w.shape[1])))


def kernel(x_prompt, x_sample, cache_k_moba, cache_v_moba, cache_k_fox, cache_v_fox, cache_lf_fox, page_table, norm_mixer, norm_ffn, moba_w_qkv, moba_w_o, fox_w_in, fox_b_f, fox_w_o, ffn_w_gu, ffn_w_down, moe_w_router, moe_w_gu, moe_w_down, norm_final):
    batch, seq, d = x_prompt.shape
    dec_b, dec_seq, _ = x_sample.shape
    depth = norm_mixer.shape[0]
    past_len = page_table.shape[1] * PAGE_SIZE
    assert d == ATT_DIM and dec_seq == 1 and past_len % MOBA_BLOCK == 0
    assert cache_k_moba.shape[2:] == (PAGE_SIZE, N_HEADS, HEAD_DIM)

    xp = x_prompt.reshape(batch * seq, d)
    xs = x_sample.reshape(dec_b, d)
    pos_p = jnp.arange(seq, dtype=jnp.int32)
    pos_s = jnp.full((dec_b,), past_len, jnp.int32)
    outs = {name: [] for name in ("km_p", "vm_p", "kf_p", "vf_p", "lf_p", "km_s", "vm_s", "kf_s", "vf_s", "lf_s")}
    heads3 = lambda a: a.reshape(dec_b, N_HEADS, HEAD_DIM)

    for i in range(depth):
        j = i // 2
        last = i == depth - 1
        if i % 2 == 0:
            w_qkv = moba_w_qkv[j].astype(BF16)
            w_o = moba_w_o[j].astype(BF16)
            pp = _in_proj(xp, norm_mixer[i], w_qkv, prompt=True, rope_pos=pos_p, kmean=True)
            ps = _in_proj(xs, norm_mixer[i], w_qkv, prompt=False, rope_pos=pos_s)
            op = _moba_prompt_attn(pp["q"], pp["kb"], pp["vb"], pp["km"], batch, seq)
            q3 = heads3(ps["q"])
            sel = _moba_dec_select(q3, cache_k_moba, page_table, j)
            page_slot = (sel[:, :, :MOBA_TOPK, None] * PAGES_PER_BLOCK
                         + jnp.arange(PAGES_PER_BLOCK, dtype=jnp.int32)).reshape(dec_b, -1)
            sel_pages = jnp.take_along_axis(page_table, page_slot, axis=1).reshape(-1)
            os_ = _moba_dec_attn(q3, heads3(ps["k"]), heads3(ps["v"]), cache_k_moba, cache_v_moba, sel_pages, j)
            outs["km_p"].append(pp["k"]); outs["vm_p"].append(pp["v"])
            outs["km_s"].append(ps["k"]); outs["vm_s"].append(ps["v"])
            wgu, wd = ffn_w_gu[j].astype(BF16), ffn_w_down[j].astype(BF16)
            xp = _dense_ffn(xp, op, w_o, norm_ffn[i], wgu, wd, norm_final, last)
            xs = _dense_ffn(xs, os_.reshape(dec_b, ATT_DIM), w_o, norm_ffn[i], wgu, wd, norm_final, last)
        else:
            w_in = fox_w_in[j]
            w_qkv = w_in[:, :3 * ATT_DIM].astype(BF16)
            w_f = _pad_lanes(w_in[:, 3 * ATT_DIM:]).astype(BF16)
            b_f = _pad_lanes(fox_b_f[j].reshape(1, N_HEADS))
            w_o = fox_w_o[j].astype(BF16)
            pp = _in_proj(xp, norm_mixer[i], w_qkv, prompt=True, gate_w=w_f, gate_b=b_f, q_dtype=BF16)
            ps = _in_proj(xs, norm_mixer[i], w_qkv, prompt=False, gate_w=w_f, gate_b=b_f)
            op = _fox_prompt_attn(pp["q"], pp["kb"], pp["vb"], pp["lf"], batch, seq)
            os_ = _fox_dec_attn(heads3(ps["q"]), heads3(ps["k"]), heads3(ps["v"]), ps["lf"],
                                cache_k_fox, cache_v_fox, cache_lf_fox, page_table, j)
            outs["kf_p"].append(pp["k"]); outs["vf_p"].append(pp["v"]); outs["lf_p"].append(pp["lf"][:, :N_HEADS])
            outs["kf_s"].append(ps["k"]); outs["vf_s"].append(ps["v"]); outs["lf_s"].append(ps["lf"][:, :N_HEADS])
            wr = _pad_lanes(moe_w_router[j])
            wgu, wd = moe_w_gu[j].astype(BF16), moe_w_down[j].astype(BF16)
            xp = _moe_ffn(xp, op, w_o, norm_ffn[i], wr, wgu, wd, norm_final, last)
            xs = _moe_ffn(xs, os_.reshape(dec_b, ATT_DIM), w_o, norm_ffn[i], wr, wgu, wd, norm_final, last)

    def rows(name, b, t, tail):
        return jnp.stack(outs[name]).reshape((len(outs[name]), b, t) + tail)

    kv_tail = (N_HEADS, HEAD_DIM)
    return (xp.reshape(batch, seq, d), xs.reshape(dec_b, dec_seq, d),
            rows("km_p", batch, seq, kv_tail), rows("vm_p", batch, seq, kv_tail),
            rows("kf_p", batch, seq, kv_tail), rows("vf_p", batch, seq, kv_tail),
            rows("lf_p", batch, seq, (N_HEADS,)),
            rows("km_s", dec_b, dec_seq, kv_tail), rows("vm_s", dec_b, dec_seq, kv_tail),
            rows("kf_s", dec_b, dec_seq, kv_tail), rows("vf_s", dec_b, dec_seq, kv_tail),
            rows("lf_s", dec_b, dec_seq, (N_HEADS,)))
```
